```python
import jax, jax.numpy as jnp
from jax import lax
import numpy as np

D_MODEL = 1024
BATCH = 8
SEQ = 2048
DEPTH = 2
DEC_BATCH = 128
DEC_SEQ = 4
PAST_LEN = 16384
PAGE_SIZE = 128

D_A = D_MODEL // 2
D_B = D_MODEL // 2
CONV_A_WIDTH = 31
CONV_B_WIDTH = 3
POOL_WINDOWS = (2, 4, 8, 16)
N_POOL_GROUPS = len(POOL_WINDOWS)
POOL_GROUP = D_MODEL // N_POOL_GROUPS
POOL_HIST = max(POOL_WINDOWS) - 1
D_FF = 256 * ((8 * D_MODEL // 3 + 255) // 256)
N_EVEN = (DEPTH + 1) // 2
N_ODD = DEPTH // 2
EPS = 1e-6

kernel_name = "hybrid_conv_pool_macaron_decode_step"


def rmsnorm(x, g):
    xf = x.astype(jnp.float32)
    y = xf * lax.rsqrt(jnp.mean(xf * xf, axis=-1, keepdims=True) + EPS)
    return (y * g.astype(jnp.float32)).astype(x.dtype)


def layernorm(x, g, b):
    xf = x.astype(jnp.float32)
    mu = jnp.mean(xf, axis=-1, keepdims=True)
    xc = xf - mu
    y = xc * lax.rsqrt(jnp.mean(xc * xc, axis=-1, keepdims=True) + EPS)
    return (y * g.astype(jnp.float32) + b.astype(jnp.float32)).astype(x.dtype)


def swiglu(h, w_in, w_out):
    gate, up = jnp.split(h @ w_in, 2, axis=-1)
    return (jax.nn.silu(gate) * up) @ w_out


def causal_dwconv(ext, w):
    return lax.conv_general_dilated(
        ext, w[:, None, :].astype(ext.dtype), window_strides=(1,), padding='VALID',
        dimension_numbers=('NWC', 'WIO', 'NWC'), feature_group_count=ext.shape[-1])


def conv_heads_mixer(h, hist_a, hist_b, w_in, conv_a_w, conv_a_b, ln_a_g, ln_a_b, conv_b_w, w_out):
    z = h @ w_in
    a_val, a_gate, b_gate, c_gate, v = jnp.split(z, [D_A, 2 * D_A, 2 * D_A + D_B, 2 * D_A + 2 * D_B], axis=-1)
    ua = a_val * jax.nn.sigmoid(a_gate)
    ext_a = jnp.concatenate([hist_a.astype(ua.dtype), ua], axis=1)
    ya = causal_dwconv(ext_a, conv_a_w) + conv_a_b.astype(ua.dtype)
    ya = jax.nn.silu(layernorm(ya, ln_a_g, ln_a_b))
    ub = c_gate * v
    ext_b = jnp.concatenate([hist_b.astype(ub.dtype), ub], axis=1)
    yb = b_gate * causal_dwconv(ext_b, conv_b_w)
    out = jnp.concatenate([ya, yb], axis=-1) @ w_out
    return out, ext_a[:, -(CONV_A_WIDTH - 1):], ext_b[:, -(CONV_B_WIDTH - 1):]


def pool_mixer(h, hist, pos0, pool_w, pool_b, pool_scale):
    T = h.shape[1]
    ext = jnp.concatenate([hist.astype(h.dtype), h], axis=1)
    cs = jnp.pad(jnp.cumsum(ext.astype(jnp.float32), axis=1), ((0, 0), (1, 0), (0, 0)))
    pos = (pos0 + jnp.arange(T)).astype(jnp.float32)
    means = []
    for g, w in enumerate(POOL_WINDOWS):
        sl = slice(g * POOL_GROUP, (g + 1) * POOL_GROUP)
        end = cs[:, POOL_HIST + 1:POOL_HIST + 1 + T, sl]
        start = cs[:, POOL_HIST + 1 - w:POOL_HIST + 1 - w + T, sl]
        count = jnp.minimum(pos + 1.0, float(w))[None, :, None]
        means.append((end - start) / count)
    pooled = jnp.concatenate(means, axis=-1) - h.astype(jnp.float32)
    pg = pooled.reshape(pooled.shape[0], T, N_POOL_GROUPS, POOL_GROUP)
    mixed = jnp.einsum('btgc,gcd->btgd', pg, pool_w.astype(jnp.float32)).reshape(pooled.shape)
    out = (mixed + pool_b.astype(jnp.float32)) * pool_scale.astype(jnp.float32)
    return out.astype(h.dtype), ext[:, -POOL_HIST:]


def setup_inputs(seed: int = 0) -> dict:
    key = jax.random.key(seed)
    ks = jax.random.split(key, 24)
    f32 = jnp.float32
    nrm = lambda k, shape, s: jax.random.normal(k, shape, f32) * s
    return {
        "x_prompt": nrm(ks[0], (BATCH, SEQ, D_MODEL), 1.0),
        "x_sample": nrm(ks[1], (DEC_BATCH, DEC_SEQ, D_MODEL), 1.0),
        "state_conv_a": nrm(ks[2], (N_EVEN, DEC_BATCH, CONV_A_WIDTH - 1, D_A), 0.5),
        "state_conv_b": nrm(ks[3], (N_EVEN, DEC_BATCH, CONV_B_WIDTH - 1, D_B), 0.5),
        "state_pool": nrm(ks[4], (N_ODD, DEC_BATCH, POOL_HIST, D_MODEL), 1.0),
        "norm_g": 1.0 + nrm(ks[5], (DEPTH, 3, D_MODEL), 0.02),
        "ffn_w_in": nrm(ks[6], (DEPTH, 2, D_MODEL, 2 * D_FF), D_MODEL ** -0.5),
        "ffn_w_out": nrm(ks[7], (DEPTH, 2, D_FF, D_MODEL), D_FF ** -0.5),
        "mix_w_in": nrm(ks[8], (N_EVEN, D_MODEL, 2 * D_A + 3 * D_B), D_MODEL ** -0.5),
        "conv_a_w": nrm(ks[9], (N_EVEN, CONV_A_WIDTH, D_A), CONV_A_WIDTH ** -0.5),
        "conv_a_b": nrm(ks[10], (N_EVEN, D_A), 0.02),
        "ln_a_g": 1.0 + nrm(ks[11], (N_EVEN, D_A), 0.02),
        "ln_a_b": nrm(ks[12], (N_EVEN, D_A), 0.02),
        "conv_b_w": nrm(ks[13], (N_EVEN, CONV_B_WIDTH, D_B), CONV_B_WIDTH ** -0.5),
        "mix_w_out": nrm(ks[14], (N_EVEN, D_A + D_B, D_MODEL), (D_A + D_B) ** -0.5),
        "pool_w": nrm(ks[15], (N_ODD, N_POOL_GROUPS, POOL_GROUP, POOL_GROUP), POOL_GROUP ** -0.5),
        "pool_b": nrm(ks[16], (N_ODD, D_MODEL), 0.02),
        "pool_scale": 0.5 + nrm(ks[17], (N_ODD, D_MODEL), 0.05),
        "final_norm_g": 1.0 + nrm(ks[18], (D_MODEL,), 0.02),
    }


def reference(x_prompt, x_sample, state_conv_a, state_conv_b, state_pool,
              norm_g, ffn_w_in, ffn_w_out, mix_w_in, conv_a_w, conv_a_b, ln_a_g, ln_a_b,
              conv_b_w, mix_w_out, pool_w, pool_b, pool_scale, final_norm_g):

    def run_group(x, hist_a, hist_b, hist_pool, pos0):
        new_a, new_b, new_p = [], [], []
        for l in range(DEPTH):
            x = x + 0.5 * swiglu(rmsnorm(x, norm_g[l, 0]), ffn_w_in[l, 0], ffn_w_out[l, 0])
            h = rmsnorm(x, norm_g[l, 1])
            if l % 2 == 0:
                e = l // 2
                m, ha, hb = conv_heads_mixer(h, hist_a[e], hist_b[e], mix_w_in[e], conv_a_w[e], conv_a_b[e],
                                             ln_a_g[e], ln_a_b[e], conv_b_w[e], mix_w_out[e])
                new_a.append(ha)
                new_b.append(hb)
            else:
                o = l // 2
                m, hp = pool_mixer(h, hist_pool[o], pos0, pool_w[o], pool_b[o], pool_scale[o])
                new_p.append(hp)
            x = x + m
            x = x + 0.5 * swiglu(rmsnorm(x, norm_g[l, 2]), ffn_w_in[l, 1], ffn_w_out[l, 1])
        y = rmsnorm(x, final_norm_g)
        return y, jnp.stack(new_a, axis=0), jnp.stack(new_b, axis=0), jnp.stack(new_p, axis=0)

    dt = x_prompt.dtype
    zero_a = jnp.zeros((N_EVEN, BATCH, CONV_A_WIDTH - 1, D_A), dt)
    zero_b = jnp.zeros((N_EVEN, BATCH, CONV_B_WIDTH - 1, D_B), dt)
    zero_p = jnp.zeros((N_ODD, BATCH, POOL_HIST, D_MODEL), dt)
    y_prompt, pa, pb, pp = run_group(x_prompt, zero_a, zero_b, zero_p, 0)
    y_sample, sa, sb, sp = run_group(x_sample, state_conv_a, state_conv_b, state_pool, PAST_LEN)
    return (y_prompt, y_sample, pa, pb, pp, sa, sb, sp)
```

```python
import functools

import jax
import jax.numpy as jnp
from jax.experimental import pallas as pl
from jax.experimental.pallas import tpu as pltpu

F32 = jnp.float32
BF16 = jnp.bfloat16

EPS = 1e-6
CONV_A_WIDTH = 31
CONV_B_WIDTH = 3
POOL_WINDOWS = (2, 4, 8, 16)
POOL_HIST = max(POOL_WINDOWS) - 1
PAST_LEN = 16384

SUBLANES = 8
LANES = 128
VMEM_LIMIT_BYTES = 56 * 1024 * 1024

FFN_TILE = 512
FFN_CHUNK = 256
MIX_TILE = 512
CONV_ROWS = 64


def _rms(x, g):
    ms = jnp.mean(x * x, axis=-1, keepdims=True)
    return x * jax.lax.rsqrt(ms + EPS) * g


def _dot(a, b):
    return jnp.dot(a, b, preferred_element_type=F32)


def _const_spec(shape):
    nd = len(shape)
    return pl.BlockSpec(shape, lambda *_: (0,) * nd, pipeline_mode=pl.Buffered(1))


def _params(sem):
    return pltpu.CompilerParams(dimension_semantics=sem, vmem_limit_bytes=VMEM_LIMIT_BYTES)


def _ffn_kernel(*refs, d_ff, chunk, final):
    if final:
        x_ref, g_ref, win_ref, wout_ref, fg_ref, o_ref = refs
    else:
        x_ref, g_ref, win_ref, wout_ref, o_ref = refs
    x = x_ref[...]
    h = _rms(x, g_ref[...]).astype(BF16)
    acc = None
    for c in range(d_ff // chunk):
        lo = c * chunk
        gate = _dot(h, win_ref[:, lo:lo + chunk])
        up = _dot(h, win_ref[:, d_ff + lo:d_ff + lo + chunk])
        a = (gate * jax.nn.sigmoid(gate) * up).astype(BF16)
        p = _dot(a, wout_ref[lo:lo + chunk, :])
        acc = p if acc is None else acc + p
    y = x + 0.5 * acc
    if final:
        y = _rms(y, fg_ref[...])
    o_ref[...] = y


def _ffn(x, g, w_in, w_out, final_g=None):
    n, d = x.shape
    d_ff = w_out.shape[0]
    tile = min(FFN_TILE, n)
    assert n % tile == 0 and d_ff % FFN_CHUNK == 0
    final = final_g is not None
    ins = [x, g.reshape(1, d), w_in, w_out]
    specs = [pl.BlockSpec((tile, d), lambda i: (i, 0)), _const_spec((1, d)),
             _const_spec(w_in.shape), _const_spec(w_out.shape)]
    if final:
        ins.append(final_g.reshape(1, d))
        specs.append(_const_spec((1, d)))
    return pl.pallas_call(
        functools.partial(_ffn_kernel, d_ff=d_ff, chunk=FFN_CHUNK, final=final),
        out_shape=jax.ShapeDtypeStruct((n, d), F32),
        grid=(n // tile,),
        in_specs=specs,
        out_specs=pl.BlockSpec((tile, d), lambda i: (i, 0)),
        compiler_params=_params(("arbitrary",)),
        name="ffn_final" if final else "ffn",
    )(*ins)


def _layernorm_silu(y, g, b):
    mu = jnp.mean(y, axis=-1, keepdims=True)
    yc = y - mu
    yn = yc * jax.lax.rsqrt(jnp.mean(yc * yc, axis=-1, keepdims=True) + EPS) * g + b
    return yn * jax.nn.sigmoid(yn)


def _split_z(z, d_a, d_b):
    a_val = z[:, :d_a]
    a_gate = z[:, d_a:2 * d_a]
    b_gate = z[:, 2 * d_a:2 * d_a + d_b]
    c_gate = z[:, 2 * d_a + d_b:2 * d_a + 2 * d_b]
    v = z[:, 2 * d_a + 2 * d_b:]
    return a_val, a_gate, b_gate, c_gate, v


A_PAD = 32
B_PAD = 8


def _mix_ab_prompt_kernel(x_ref, g_ref, win_ref, caw_ref, cab_ref, lng_ref, lnb_ref, cbw_ref,
                          wout_ref, o_ref, na_ref, nb_ref, exta, extb, ycat, *, tile, d_a, d_b):
    t = pl.program_id(1)

    @pl.when(t == 0)
    def _():
        exta[0:A_PAD, :] = jnp.zeros((A_PAD, d_a), F32)
        extb[0:B_PAD, :] = jnp.zeros((B_PAD, d_b), F32)

    x = x_ref[...]
    h = _rms(x, g_ref[...]).astype(BF16)
    z = _dot(h, win_ref[...])
    a_val, a_gate, b_gate, c_gate, v = _split_z(z, d_a, d_b)
    exta[A_PAD:A_PAD + tile, :] = a_val * jax.nn.sigmoid(a_gate)
    extb[B_PAD:B_PAD + tile, :] = c_gate * v

    off_a = A_PAD - (CONV_A_WIDTH - 1)
    off_b = B_PAD - (CONV_B_WIDTH - 1)
    for r0 in range(0, tile, CONV_ROWS):
        for c0 in range(0, d_a, LANES):
            cs = slice(c0, c0 + LANES)
            acc = jnp.broadcast_to(cab_ref[:, cs], (CONV_ROWS, LANES))
            for k in range(CONV_A_WIDTH):
                acc = acc + caw_ref[k:k + 1, cs] * exta[r0 + off_a + k:r0 + off_a + k + CONV_ROWS, cs]
            ycat[r0:r0 + CONV_ROWS, c0:c0 + LANES] = acc
        for c0 in range(0, d_b, LANES):
            cs = slice(c0, c0 + LANES)
            acc = None
            for k in range(CONV_B_WIDTH):
                term = cbw_ref[k:k + 1, cs] * extb[r0 + off_b + k:r0 + off_b + k + CONV_ROWS, cs]
                acc = term if acc is None else acc + term
            ycat[r0:r0 + CONV_ROWS, d_a + c0:d_a + c0 + LANES] = acc

    ya = _layernorm_silu(ycat[:, :d_a], lng_ref[...], lnb_ref[...])
    yb = b_gate * ycat[:, d_a:]
    out = _dot(ya.astype(BF16), wout_ref[:d_a, :]) + _dot(yb.astype(BF16), wout_ref[d_a:, :])
    o_ref[...] = x + out

    exta[0:A_PAD, :] = exta[tile:tile + A_PAD, :]
    extb[0:B_PAD, :] = extb[tile:tile + B_PAD, :]

    @pl.when(t == pl.num_programs(1) - 1)
    def _():
        na_ref[...] = exta[off_a:A_PAD, :]
        nb_ref[...] = extb[off_b:B_PAD, :]


def _mix_ab_prompt(x, g, w_in, caw, cab, lng, lnb, cbw, w_out):
    b, t, d = x.shape
    d_a = caw.shape[1]
    d_b = cbw.shape[1]
    tile = min(MIX_TILE, t)
    assert t % tile == 0 and tile >= A_PAD and tile % CONV_ROWS == 0
    row = lambda a: a.reshape(1, -1)
    return pl.pallas_call(
        functools.partial(_mix_ab_prompt_kernel, tile=tile, d_a=d_a, d_b=d_b),
        out_shape=(jax.ShapeDtypeStruct((b, t, d), F32),
                   jax.ShapeDtypeStruct((b, CONV_A_WIDTH - 1, d_a), F32),
                   jax.ShapeDtypeStruct((b, CONV_B_WIDTH - 1, d_b), F32)),
        grid=(b, t // tile),
        in_specs=[pl.BlockSpec((None, tile, d), lambda i, j: (i, j, 0)),
                  _const_spec((1, d)), _const_spec(w_in.shape), _const_spec(caw.shape),
                  _const_spec((1, d_a)), _const_spec((1, d_a)), _const_spec((1, d_a)),
                  _const_spec(cbw.shape), _const_spec(w_out.shape)],
        out_specs=(pl.BlockSpec((None, tile, d), lambda i, j: (i, j, 0)),
                   pl.BlockSpec((None, CONV_A_WIDTH - 1, d_a), lambda i, j: (i, 0, 0)),
                   pl.BlockSpec((None, CONV_B_WIDTH - 1, d_b), lambda i, j: (i, 0, 0))),
        scratch_shapes=[pltpu.VMEM((A_PAD + tile, d_a), F32),
                        pltpu.VMEM((B_PAD + tile, d_b), F32),
                        pltpu.VMEM((tile, d_a + d_b), F32)],
        compiler_params=_params(("arbitrary", "arbitrary")),
        name="mix_ab_prompt",
    )(x, row(g), w_in, caw, row(cab), row(lng), row(lnb), cbw, w_out)


def _mix_ab_sample_kernel(x_ref, ha_ref, hb_ref, g_ref, win_ref, caw_ref, cab_ref, lng_ref, lnb_ref,
                          cbw_ref, wout_ref, o_ref, na_ref, nb_ref, ua_s, ub_s, ycat,
                          *, n_seq, t_new, d_a, d_b):
    hist_a = CONV_A_WIDTH - 1
    hist_b = CONV_B_WIDTH - 1

    def rows(t):
        return slice(t * n_seq, (t + 1) * n_seq)

    def ext_a(j):
        return ha_ref[j] if j < hist_a else ua_s[rows(j - hist_a), :]

    def ext_b(j):
        return hb_ref[j] if j < hist_b else ub_s[rows(j - hist_b), :]

    x = x_ref[...]
    h = _rms(x, g_ref[...]).astype(BF16)
    z = _dot(h, win_ref[...])
    a_val, a_gate, b_gate, c_gate, v = _split_z(z, d_a, d_b)
    ua_s[...] = a_val * jax.nn.sigmoid(a_gate)
    ub_s[...] = c_gate * v

    for t in range(t_new):
        acc = jnp.broadcast_to(cab_ref[...], (n_seq, d_a))
        for k in range(CONV_A_WIDTH):
            acc = acc + caw_ref[k:k + 1, :] * ext_a(t + k)
        ycat[rows(t), 0:d_a] = _layernorm_silu(acc, lng_ref[...], lnb_ref[...])
        acc = None
        for k in range(CONV_B_WIDTH):
            term = cbw_ref[k:k + 1, :] * ext_b(t + k)
            acc = term if acc is None else acc + term
        ycat[rows(t), d_a:d_a + d_b] = acc

    ya = ycat[:, :d_a]
    yb = b_gate * ycat[:, d_a:]
    out = _dot(ya.astype(BF16), wout_ref[:d_a, :]) + _dot(yb.astype(BF16), wout_ref[d_a:, :])
    o_ref[...] = x + out

    for j in range(hist_a):
        na_ref[j] = ext_a(j + t_new)
    for j in range(hist_b):
        nb_ref[j] = ext_b(j + t_new)


def _mix_ab_sample(x, hist_a, hist_b, g, w_in, caw, cab, lng, lnb, cbw, w_out):
    n, d = x.shape
    b = hist_a.shape[1]
    d_a = caw.shape[1]
    d_b = cbw.shape[1]
    row = lambda a: a.reshape(1, -1)
    ins = [x, hist_a, hist_b, row(g), w_in, caw, row(cab), row(lng), row(lnb), cbw, w_out]
    return pl.pallas_call(
        functools.partial(_mix_ab_sample_kernel, n_seq=b, t_new=n // b, d_a=d_a, d_b=d_b),
        out_shape=(jax.ShapeDtypeStruct((n, d), F32),
                   jax.ShapeDtypeStruct(hist_a.shape, F32),
                   jax.ShapeDtypeStruct(hist_b.shape, F32)),
        scratch_shapes=[pltpu.VMEM((n, d_a), F32), pltpu.VMEM((n, d_b), F32),
                        pltpu.VMEM((n, d_a + d_b), F32)],
        compiler_params=pltpu.CompilerParams(vmem_limit_bytes=VMEM_LIMIT_BYTES),
        name="mix_ab_sample",
    )(*ins)


P_PAD = 16


def _pool_mix(pooled, pw_ref, pb_ref, ps_ref, group):
    outs = []
    for gi in range(len(POOL_WINDOWS)):
        cs = slice(gi * group, (gi + 1) * group)
        outs.append(_dot(pooled[:, cs].astype(BF16), pw_ref[gi]))
    mixed = jnp.concatenate(outs, axis=-1)
    return (mixed + pb_ref[...]) * ps_ref[...]


def _pool_prompt_kernel(x_ref, g_ref, pw_ref, pb_ref, ps_ref, o_ref, np_ref, exth, pooled_s,
                        *, tile, d, group):
    t = pl.program_id(1)

    @pl.when(t == 0)
    def _():
        exth[0:P_PAD, :] = jnp.zeros((P_PAD, d), F32)

    x = x_ref[...]
    h = _rms(x, g_ref[...])
    exth[P_PAD:P_PAD + tile, :] = h

    pos = (t * tile + jax.lax.broadcasted_iota(jnp.int32, (tile, group), 0)).astype(F32)
    for gi, w in enumerate(POOL_WINDOWS):
        cs = slice(gi * group, (gi + 1) * group)
        s = exth[P_PAD:P_PAD + tile, cs]
        for j in range(1, w):
            s = s + exth[P_PAD - j:P_PAD - j + tile, cs]
        count = jnp.minimum(pos + 1.0, float(w))
        pooled_s[:, cs] = s / count - exth[P_PAD:P_PAD + tile, cs]

    o_ref[...] = x + _pool_mix(pooled_s[...], pw_ref, pb_ref, ps_ref, group)

    exth[0:P_PAD, :] = exth[tile:tile + P_PAD, :]

    @pl.when(t == pl.num_programs(1) - 1)
    def _():
        np_ref[...] = exth[P_PAD - POOL_HIST:P_PAD, :]


def _pool_prompt(x, g, pw, pb, ps):
    b, t, d = x.shape
    group = pw.shape[-1]
    tile = min(MIX_TILE, t)
    assert t % tile == 0 and tile >= P_PAD
    row = lambda a: a.reshape(1, -1)
    return pl.pallas_call(
        functools.partial(_pool_prompt_kernel, tile=tile, d=d, group=group),
        out_shape=(jax.ShapeDtypeStruct((b, t, d), F32),
                   jax.ShapeDtypeStruct((b, POOL_HIST, d), F32)),
        grid=(b, t // tile),
        in_specs=[pl.BlockSpec((None, tile, d), lambda i, j: (i, j, 0)),
                  _const_spec((1, d)), _const_spec(pw.shape), _const_spec((1, d)),
                  _const_spec((1, d))],
        out_specs=(pl.BlockSpec((None, tile, d), lambda i, j: (i, j, 0)),
                   pl.BlockSpec((None, POOL_HIST, d), lambda i, j: (i, 0, 0))),
        scratch_shapes=[pltpu.VMEM((P_PAD + tile, d), F32), pltpu.VMEM((tile, d), F32)],
        compiler_params=_params(("arbitrary", "arbitrary")),
        name="pool_prompt",
    )(x, row(g), pw, row(pb), row(ps))


def _pool_sample_kernel(x_ref, hp_ref, g_ref, pw_ref, pb_ref, ps_ref, o_ref, np_ref, h_s, pooled_s,
                        *, n_seq, t_new, pos0, group):
    def rows(t):
        return slice(t * n_seq, (t + 1) * n_seq)

    def ext(j, cs):
        return hp_ref[j, :, cs] if j < POOL_HIST else h_s[rows(j - POOL_HIST), cs]

    x = x_ref[...]
    h_s[...] = _rms(x, g_ref[...])

    for t in range(t_new):
        for gi, w in enumerate(POOL_WINDOWS):
            cs = slice(gi * group, (gi + 1) * group)
            s = ext(POOL_HIST + t, cs)
            for j in range(1, w):
                s = s + ext(POOL_HIST + t - j, cs)
            count = float(min(pos0 + t + 1, w))
            pooled_s[rows(t), cs] = s / count - ext(POOL_HIST + t, cs)

    o_ref[...] = x + _pool_mix(pooled_s[...], pw_ref, pb_ref, ps_ref, group)

    full = slice(None)
    for j in range(POOL_HIST):
        np_ref[j] = ext(j + t_new, full)


def _pool_sample(x, hist, pos0, g, pw, pb, ps):
    n, d = x.shape
    b = hist.shape[1]
    group = pw.shape[-1]
    row = lambda a: a.reshape(1, -1)
    return pl.pallas_call(
        functools.partial(_pool_sample_kernel, n_seq=b, t_new=n // b, pos0=pos0, group=group),
        out_shape=(jax.ShapeDtypeStruct((n, d), F32), jax.ShapeDtypeStruct(hist.shape, F32)),
        scratch_shapes=[pltpu.VMEM((n, d), F32), pltpu.VMEM((n, d), F32)],
        compiler_params=pltpu.CompilerParams(vmem_limit_bytes=VMEM_LIMIT_BYTES),
        name="pool_sample",
    )(x, hist, row(g), pw, row(pb), row(ps))


def kernel(x_prompt, x_sample, state_conv_a, state_conv_b, state_pool, norm_g, ffn_w_in, ffn_w_out,
           mix_w_in, conv_a_w, conv_a_b, ln_a_g, ln_a_b, conv_b_w, mix_w_out, pool_w, pool_b,
           pool_scale, final_norm_g):
    depth = norm_g.shape[0]
    d = x_prompt.shape[-1]
    bs, ts, _ = x_sample.shape
    step_major = lambda a: jnp.swapaxes(a, 0, 1)

    ffn_w_in_h = ffn_w_in.astype(BF16)
    ffn_w_out_h = ffn_w_out.astype(BF16)
    mix_w_in_h = mix_w_in.astype(BF16)
    mix_w_out_h = mix_w_out.astype(BF16)
    pool_w_h = pool_w.astype(BF16)

    def ffn(x, l, which, final_g=None):
        shp = x.shape
        y = _ffn(x.reshape(-1, d), norm_g[l, 0 if which == 0 else 2], ffn_w_in_h[l, which],
                 ffn_w_out_h[l, which], final_g)
        return y.reshape(shp)

    xp, xs = x_prompt, step_major(x_sample).reshape(ts * bs, d)
    pa, pb, pp, sa, sb, sp = [], [], [], [], [], []
    for l in range(depth):
        last = l == depth - 1
        xp = ffn(xp, l, 0)
        xs = ffn(xs, l, 0)
        if l % 2 == 0:
            e = l // 2
            wts = (norm_g[l, 1], mix_w_in_h[e], conv_a_w[e], conv_a_b[e], ln_a_g[e], ln_a_b[e],
                   conv_b_w[e], mix_w_out_h[e])
            xp, na, nb = _mix_ab_prompt(xp, *wts)
            pa.append(na)
            pb.append(nb)
            xs, na, nb = _mix_ab_sample(xs, step_major(state_conv_a[e]), step_major(state_conv_b[e]),
                                        *wts)
            sa.append(step_major(na))
            sb.append(step_major(nb))
        else:
            o = l // 2
            wts = (norm_g[l, 1], pool_w_h[o], pool_b[o], pool_scale[o])
            xp, npool = _pool_prompt(xp, *wts)
            pp.append(npool)
            xs, npool = _pool_sample(xs, step_major(state_pool[o]), PAST_LEN, *wts)
            sp.append(step_major(npool))
        xp = ffn(xp, l, 1, final_norm_g if last else None)
        xs = ffn(xs, l, 1, final_norm_g if last else None)

    xs = step_major(xs.reshape(ts, bs, d))
    st = lambda parts: jnp.stack(parts, axis=0)
    return (xp, xs, st(pa), st(pb), st(pp), st(sa), st(sb), st(sp))
```

```python
import functools

import jax
import jax.numpy as jnp
from jax.experimental import pallas as pl
from jax.experimental.pallas import tpu as pltpu

F32 = jnp.float32
BF16 = jnp.bfloat16

EPS = 1e-6
CONV_A_WIDTH = 31
CONV_B_WIDTH = 3
POOL_WINDOWS = (2, 4, 8, 16)
POOL_HIST = max(POOL_WINDOWS) - 1
PAST_LEN = 16384

SUBLANES = 8
BF16_SUBLANES = 16
LANES = 128
VMEM_LIMIT_BYTES = 56 * 1024 * 1024

FFN_TILE = 512
FFN_CHUNK = 256
MIX_TILE = 512
CONV_ROWS = 64


def _rms(x, g):
    ms = jnp.mean(x * x, axis=-1, keepdims=True)
    return x * jax.lax.rsqrt(ms + EPS) * g


def _dot(a, b):
    return jnp.dot(a, b, preferred_element_type=F32)


def _const_spec(shape):
    nd = len(shape)
    return pl.BlockSpec(shape, lambda *_: (0,) * nd, pipeline_mode=pl.Buffered(1))


def _params(sem):
    return pltpu.CompilerParams(dimension_semantics=sem, vmem_limit_bytes=VMEM_LIMIT_BYTES)


def _ffn_kernel(*refs, d_ff, chunk, final, n_cast):
    n_in = 4 + (1 if final else 0)
    x_ref, g_ref, win_ref, wout_ref = refs[:4]
    fg_ref = refs[4] if final else None
    cast_src = refs[n_in:n_in + n_cast]
    o_ref = refs[n_in + n_cast]
    cast_dst = refs[n_in + n_cast + 1:]
    for src, dst in zip(cast_src, cast_dst):
        dst[...] = src[...].astype(BF16)
    x = x_ref[...]
    h = _rms(x, g_ref[...]).astype(BF16)
    acc = None
    for c in range(d_ff // chunk):
        lo = c * chunk
        gate = _dot(h, win_ref[:, lo:lo + chunk])
        up = _dot(h, win_ref[:, d_ff + lo:d_ff + lo + chunk])
        a = (gate * jax.nn.sigmoid(gate) * up).astype(BF16)
        p = _dot(a, wout_ref[lo:lo + chunk, :])
        acc = p if acc is None else acc + p
    y = x + 0.5 * acc
    if final:
        y = _rms(y, fg_ref[...])
    o_ref[...] = y


def _cast_specs(w, sel, n_steps):
    r, c = w.shape[len(sel):]
    n_blocks = n_steps
    while r % n_blocks or (r // n_blocks) % BF16_SUBLANES:
        assert n_blocks % 2 == 0, (r, n_steps)
        n_blocks //= 2
    rep = n_steps // n_blocks
    rows = r // n_blocks
    in_spec = pl.BlockSpec((None,) * len(sel) + (rows, c), lambda i: tuple(sel) + (i // rep, 0))
    out_spec = pl.BlockSpec((rows, c), lambda i: (i // rep, 0))
    return in_spec, out_spec, jax.ShapeDtypeStruct((r, c), BF16)


def _ffn(x, g, w_in, w_out, final_g=None, cast=()):
    n, d = x.shape
    d_ff = w_out.shape[0]
    tile = min(FFN_TILE, n)
    assert n % tile == 0 and d_ff % FFN_CHUNK == 0
    n_steps = n // tile
    final = final_g is not None
    ins = [x, g.reshape(1, d), w_in, w_out]
    specs = [pl.BlockSpec((tile, d), lambda i: (i, 0)), _const_spec((1, d)),
             _const_spec(w_in.shape), _const_spec(w_out.shape)]
    if final:
        ins.append(final_g.reshape(1, d))
        specs.append(_const_spec((1, d)))
    out_shapes = [jax.ShapeDtypeStruct((n, d), F32)]
    out_specs = [pl.BlockSpec((tile, d), lambda i: (i, 0))]
    for w, sel in cast:
        in_spec, out_spec, out_shape = _cast_specs(w, sel, n_steps)
        ins.append(w)
        specs.append(in_spec)
        out_specs.append(out_spec)
        out_shapes.append(out_shape)
    outs = pl.pallas_call(
        functools.partial(_ffn_kernel, d_ff=d_ff, chunk=FFN_CHUNK, final=final, n_cast=len(cast)),
        out_shape=tuple(out_shapes),
        grid=(n_steps,),
        in_specs=specs,
        out_specs=tuple(out_specs),
        compiler_params=_params(("arbitrary",)),
        name="ffn_final" if final else "ffn",
    )(*ins)
    return outs[0] if not cast else outs


def _layernorm_silu(y, g, b):
    mu = jnp.mean(y, axis=-1, keepdims=True)
    yc = y - mu
    yn = yc * jax.lax.rsqrt(jnp.mean(yc * yc, axis=-1, keepdims=True) + EPS) * g + b
    return yn * jax.nn.sigmoid(yn)


def _split_z(z, d_a, d_b):
    a_val = z[:, :d_a]
    a_gate = z[:, d_a:2 * d_a]
    b_gate = z[:, 2 * d_a:2 * d_a + d_b]
    c_gate = z[:, 2 * d_a + d_b:2 * d_a + 2 * d_b]
    v = z[:, 2 * d_a + 2 * d_b:]
    return a_val, a_gate, b_gate, c_gate, v


A_PAD = 32
B_PAD = 8
SKEW_TAPS = CONV_A_WIDTH + SUBLANES - 1


def _mix_ab_prompt_kernel(x_ref, g_ref, win_ref, caw_ref, cab_ref, lng_ref, lnb_ref, cbw_ref,
                          wout_ref, o_ref, na_ref, nb_ref, exta, extb, ycat, wskew,
                          *, tile, d_a, d_b):
    t = pl.program_id(1)

    @pl.when((pl.program_id(0) == 0) & (t == 0))
    def _():
        wskew[...] = jnp.zeros(wskew.shape, F32)
        for dd in range(SKEW_TAPS):
            for p in range(SUBLANES):
                if 0 <= dd - p < CONV_A_WIDTH:
                    wskew[dd, p:p + 1, :] = caw_ref[dd - p:dd - p + 1, :]

    @pl.when(t == 0)
    def _():
        exta[0:A_PAD, :] = jnp.zeros((A_PAD, d_a), F32)
        extb[0:B_PAD, :] = jnp.zeros((B_PAD, d_b), F32)

    x = x_ref[...]
    h = _rms(x, g_ref[...]).astype(BF16)
    z = _dot(h, win_ref[...])
    a_val, a_gate, b_gate, c_gate, v = _split_z(z, d_a, d_b)
    exta[A_PAD:A_PAD + tile, :] = a_val * jax.nn.sigmoid(a_gate)
    extb[B_PAD:B_PAD + tile, :] = c_gate * v

    off_a = A_PAD - (CONV_A_WIDTH - 1)
    off_b = B_PAD - (CONV_B_WIDTH - 1)
    for r0 in range(0, tile, CONV_ROWS):
        for c0 in range(0, d_a, LANES):
            cs = slice(c0, c0 + LANES)
            n_acc = CONV_ROWS // SUBLANES
            accs = [jnp.broadcast_to(cab_ref[:, cs], (SUBLANES, LANES))] * n_acc
            for dd in range(SKEW_TAPS):
                wv = wskew[dd, :, cs]
                for m in range(n_acc):
                    src = r0 + SUBLANES * m + off_a + dd
                    accs[m] = accs[m] + wv * jnp.broadcast_to(exta[src:src + 1, cs], (SUBLANES, LANES))
            for m in range(n_acc):
                ycat[r0 + SUBLANES * m:r0 + SUBLANES * (m + 1), cs] = accs[m]
        for c0 in range(0, d_b, LANES):
            cs = slice(c0, c0 + LANES)
            acc = None
            for k in range(CONV_B_WIDTH):
                term = cbw_ref[k:k + 1, cs] * extb[r0 + off_b + k:r0 + off_b + k + CONV_ROWS, cs]
                acc = term if acc is None else acc + term
            ycat[r0:r0 + CONV_ROWS, d_a + c0:d_a + c0 + LANES] = acc

    ya = _layernorm_silu(ycat[:, :d_a], lng_ref[...], lnb_ref[...])
    yb = b_gate * ycat[:, d_a:]
    out = _dot(ya.astype(BF16), wout_ref[:d_a, :]) + _dot(yb.astype(BF16), wout_ref[d_a:, :])
    o_ref[...] = x + out

    exta[0:A_PAD, :] = exta[tile:tile + A_PAD, :]
    extb[0:B_PAD, :] = extb[tile:tile + B_PAD, :]

    @pl.when(t == pl.num_programs(1) - 1)
    def _():
        na_ref[...] = exta[off_a:A_PAD, :]
        nb_ref[...] = extb[off_b:B_PAD, :]


def _mix_ab_prompt(x, g, w_in, caw, cab, lng, lnb, cbw, w_out):
    b, t, d = x.shape
    d_a = caw.shape[1]
    d_b = cbw.shape[1]
    tile = min(MIX_TILE, t)
    assert t % tile == 0 and tile >= A_PAD and tile % CONV_ROWS == 0
    row = lambda a: a.reshape(1, -1)
    return pl.pallas_call(
        functools.partial(_mix_ab_prompt_kernel, tile=tile, d_a=d_a, d_b=d_b),
        out_shape=(jax.ShapeDtypeStruct((b, t, d), F32),
                   jax.ShapeDtypeStruct((b, CONV_A_WIDTH - 1, d_a), F32),
                   jax.ShapeDtypeStruct((b, CONV_B_WIDTH - 1, d_b), F32)),
        grid=(b, t // tile),
        in_specs=[pl.BlockSpec((None, tile, d), lambda i, j: (i, j, 0)),
                  _const_spec((1, d)), _const_spec(w_in.shape), _const_spec(caw.shape),
                  _const_spec((1, d_a)), _const_spec((1, d_a)), _const_spec((1, d_a)),
                  _const_spec(cbw.shape), _const_spec(w_out.shape)],
        out_specs=(pl.BlockSpec((None, tile, d), lambda i, j: (i, j, 0)),
                   pl.BlockSpec((None, CONV_A_WIDTH - 1, d_a), lambda i, j: (i, 0, 0)),
                   pl.BlockSpec((None, CONV_B_WIDTH - 1, d_b), lambda i, j: (i, 0, 0))),
        scratch_shapes=[pltpu.VMEM((A_PAD + tile, d_a), F32),
                        pltpu.VMEM((B_PAD + tile, d_b), F32),
                        pltpu.VMEM((tile, d_a + d_b), F32),
                        pltpu.VMEM((SKEW_TAPS, SUBLANES, d_a), F32)],
        compiler_params=_params(("arbitrary", "arbitrary")),
        name="mix_ab_prompt",
    )(x, row(g), w_in, caw, row(cab), row(lng), row(lnb), cbw, w_out)


def _mix_ab_sample_kernel(x_ref, ha_ref, hb_ref, g_ref, win_ref, caw_ref, cab_ref, lng_ref, lnb_ref,
                          cbw_ref, wout_ref, o_ref, na_ref, nb_ref, ua_s, ub_s, ycat,
                          *, n_seq, t_new, d_a, d_b):
    hist_a = CONV_A_WIDTH - 1
    hist_b = CONV_B_WIDTH - 1

    def rows(t):
        return slice(t * n_seq, (t + 1) * n_seq)

    def ext_a(j):
        return ha_ref[j] if j < hist_a else ua_s[rows(j - hist_a), :]

    def ext_b(j):
        return hb_ref[j] if j < hist_b else ub_s[rows(j - hist_b), :]

    x = x_ref[...]
    h = _rms(x, g_ref[...]).astype(BF16)
    z = _dot(h, win_ref[...])
    a_val, a_gate, b_gate, c_gate, v = _split_z(z, d_a, d_b)
    ua_s[...] = a_val * jax.nn.sigmoid(a_gate)
    ub_s[...] = c_gate * v

    for t in range(t_new):
        acc = jnp.broadcast_to(cab_ref[...], (n_seq, d_a))
        for k in range(CONV_A_WIDTH):
            acc = acc + caw_ref[k:k + 1, :] * ext_a(t + k)
        ycat[rows(t), 0:d_a] = _layernorm_silu(acc, lng_ref[...], lnb_ref[...])
        acc = None
        for k in range(CONV_B_WIDTH):
            term = cbw_ref[k:k + 1, :] * ext_b(t + k)
            acc = term if acc is None else acc + term
        ycat[rows(t), d_a:d_a + d_b] = acc

    ya = ycat[:, :d_a]
    yb = b_gate * ycat[:, d_a:]
    out = _dot(ya.astype(BF16), wout_ref[:d_a, :]) + _dot(yb.astype(BF16), wout_ref[d_a:, :])
    o_ref[...] = x + out

    for j in range(hist_a):
        na_ref[j] = ext_a(j + t_new)
    for j in range(hist_b):
        nb_ref[j] = ext_b(j + t_new)


def _mix_ab_sample(x, hist_a, hist_b, g, w_in, caw, cab, lng, lnb, cbw, w_out):
    n, d = x.shape
    b = hist_a.shape[1]
    d_a = caw.shape[1]
    d_b = cbw.shape[1]
    row = lambda a: a.reshape(1, -1)
    ins = [x, hist_a, hist_b, row(g), w_in, caw, row(cab), row(lng), row(lnb), cbw, w_out]
    return pl.pallas_call(
        functools.partial(_mix_ab_sample_kernel, n_seq=b, t_new=n // b, d_a=d_a, d_b=d_b),
        out_shape=(jax.ShapeDtypeStruct((n, d), F32),
                   jax.ShapeDtypeStruct(hist_a.shape, F32),
                   jax.ShapeDtypeStruct(hist_b.shape, F32)),
        scratch_shapes=[pltpu.VMEM((n, d_a), F32), pltpu.VMEM((n, d_b), F32),
                        pltpu.VMEM((n, d_a + d_b), F32)],
        compiler_params=pltpu.CompilerParams(vmem_limit_bytes=VMEM_LIMIT_BYTES),
        name="mix_ab_sample",
    )(*ins)


P_PAD = 16


def _pool_mix(pooled, pw_ref, pb_ref, ps_ref, group):
    outs = []
    for gi in range(len(POOL_WINDOWS)):
        cs = slice(gi * group, (gi + 1) * group)
        outs.append(_dot(pooled[:, cs].astype(BF16), pw_ref[gi]))
    mixed = jnp.concatenate(outs, axis=-1)
    return (mixed + pb_ref[...]) * ps_ref[...]


def _pool_prompt_kernel(x_ref, g_ref, pw_ref, pb_ref, ps_ref, o_ref, np_ref, exth, pooled_s,
                        *, tile, d, group):
    t = pl.program_id(1)

    @pl.when(t == 0)
    def _():
        exth[0:P_PAD, :] = jnp.zeros((P_PAD, d), F32)

    x = x_ref[...]
    h = _rms(x, g_ref[...])
    exth[P_PAD:P_PAD + tile, :] = h

    pos = (t * tile + jax.lax.broadcasted_iota(jnp.int32, (tile, group), 0)).astype(F32)
    for gi, w in enumerate(POOL_WINDOWS):
        cs = slice(gi * group, (gi + 1) * group)
        s = exth[P_PAD:P_PAD + tile, cs]
        for j in range(1, w):
            s = s + exth[P_PAD - j:P_PAD - j + tile, cs]
        count = jnp.minimum(pos + 1.0, float(w))
        pooled_s[:, cs] = s / count - exth[P_PAD:P_PAD + tile, cs]

    o_ref[...] = x + _pool_mix(pooled_s[...], pw_ref, pb_ref, ps_ref, group)

    exth[0:P_PAD, :] = exth[tile:tile + P_PAD, :]

    @pl.when(t == pl.num_programs(1) - 1)
    def _():
        np_ref[...] = exth[P_PAD - POOL_HIST:P_PAD, :]


def _pool_prompt(x, g, pw, pb, ps):
    b, t, d = x.shape
    group = pw.shape[-1]
    tile = min(MIX_TILE, t)
    assert t % tile == 0 and tile >= P_PAD
    row = lambda a: a.reshape(1, -1)
    return pl.pallas_call(
        functools.partial(_pool_prompt_kernel, tile=tile, d=d, group=group),
        out_shape=(jax.ShapeDtypeStruct((b, t, d), F32),
                   jax.ShapeDtypeStruct((b, POOL_HIST, d), F32)),
        grid=(b, t // tile),
        in_specs=[pl.BlockSpec((None, tile, d), lambda i, j: (i, j, 0)),
                  _const_spec((1, d)), _const_spec(pw.shape), _const_spec((1, d)),
                  _const_spec((1, d))],
        out_specs=(pl.BlockSpec((None, tile, d), lambda i, j: (i, j, 0)),
                   pl.BlockSpec((None, POOL_HIST, d), lambda i, j: (i, 0, 0))),
        scratch_shapes=[pltpu.VMEM((P_PAD + tile, d), F32), pltpu.VMEM((tile, d), F32)],
        compiler_params=_params(("arbitrary", "arbitrary")),
        name="pool_prompt",
    )(x, row(g), pw, row(pb), row(ps))


def _pool_sample_kernel(x_ref, hp_ref, g_ref, pw_ref, pb_ref, ps_ref, o_ref, np_ref, h_s, pooled_s,
                        *, n_seq, t_new, pos0, group):
    def rows(t):
        return slice(t * n_seq, (t + 1) * n_seq)

    def ext(j, cs):
        return hp_ref[j, :, cs] if j < POOL_HIST else h_s[rows(j - POOL_HIST), cs]

    x = x_ref[...]
    h_s[...] = _rms(x, g_ref[...])

    for t in range(t_new):
        for gi, w in enumerate(POOL_WINDOWS):
            cs = slice(gi * group, (gi + 1) * group)
            s = ext(POOL_HIST + t, cs)
            for j in range(1, w):
                s = s + ext(POOL_HIST + t - j, cs)
            count = float(min(pos0 + t + 1, w))
            pooled_s[rows(t), cs] = s / count - ext(POOL_HIST + t, cs)

    o_ref[...] = x + _pool_mix(pooled_s[...], pw_ref, pb_ref, ps_ref, group)

    full = slice(None)
    for j in range(POOL_HIST):
        np_ref[j] = ext(j + t_new, full)


def _pool_sample(x, hist, pos0, g, pw, pb, ps):
    n, d = x.shape
    b = hist.shape[1]
    group = pw.shape[-1]
    row = lambda a: a.reshape(1, -1)
    return pl.pallas_call(
        functools.partial(_pool_sample_kernel, n_seq=b, t_new=n // b, pos0=pos0, group=group),
        out_shape=(jax.ShapeDtypeStruct((n, d), F32), jax.ShapeDtypeStruct(hist.shape, F32)),
        scratch_shapes=[pltpu.VMEM((n, d), F32), pltpu.VMEM((n, d), F32)],
        compiler_params=pltpu.CompilerParams(vmem_limit_bytes=VMEM_LIMIT_BYTES),
        name="pool_sample",
    )(x, hist, row(g), pw, row(pb), row(ps))


def kernel(x_prompt, x_sample, state_conv_a, state_conv_b, state_pool, norm_g, ffn_w_in, ffn_w_out,
           mix_w_in, conv_a_w, conv_a_b, ln_a_g, ln_a_b, conv_b_w, mix_w_out, pool_w, pool_b,
           pool_scale, final_norm_g):
    depth = norm_g.shape[0]
    d = x_prompt.shape[-1]
    bs, ts, _ = x_sample.shape
    step_major = lambda a: jnp.swapaxes(a, 0, 1)

    n_even, n_odd = mix_w_in.shape[0], pool_w.shape[0]
    n_groups, group = pool_w.shape[1], pool_w.shape[2]
    pool_w2 = pool_w.reshape(n_odd, n_groups * group, group)
    mixer_casts = ([(mix_w_in, (e,)) for e in range(n_even)] + [(mix_w_out, (e,)) for e in range(n_even)]
                   + [(pool_w2, (o,)) for o in range(n_odd)])
    ffn_w = {(0, 0): (ffn_w_in[0, 0].astype(BF16), ffn_w_out[0, 0].astype(BF16))}
    mix_w_in_h, mix_w_out_h, pool_w_h = [], [], []

    def ffn_prompt(x, l, which, final_g=None):
        nxt = (l, 1) if which == 0 else (l + 1, 0)
        cast = [(ffn_w_in, nxt), (ffn_w_out, nxt)] if nxt[0] < depth else []
        if (l, which) == (0, 0):
            cast = cast + mixer_casts
        outs = _ffn(x.reshape(-1, d), norm_g[l, 2 * which], *ffn_w[(l, which)], final_g, cast)
        if not cast:
            return outs.reshape(x.shape)
        if nxt[0] < depth:
            ffn_w[nxt] = (outs[1], outs[2])
        if (l, which) == (0, 0):
            rest = outs[len(outs) - len(mixer_casts):]
            mix_w_in_h.extend(rest[:n_even])
            mix_w_out_h.extend(rest[n_even:2 * n_even])
            pool_w_h.extend(w.reshape(n_groups, group, group) for w in rest[2 * n_even:])
        return outs[0].reshape(x.shape)

    def ffn_sample(x, l, which, final_g=None):
        return _ffn(x, norm_g[l, 2 * which], *ffn_w[(l, which)], final_g)

    xp, xs = x_prompt, step_major(x_sample).reshape(ts * bs, d)
    pa, pb, pp, sa, sb, sp = [], [], [], [], [], []
    for l in range(depth):
        last = l == depth - 1
        xp = ffn_prompt(xp, l, 0)
        xs = ffn_sample(xs, l, 0)
        if l % 2 == 0:
            e = l // 2
            wts = (norm_g[l, 1], mix_w_in_h[e], conv_a_w[e], conv_a_b[e], ln_a_g[e], ln_a_b[e],
                   conv_b_w[e], mix_w_out_h[e])
            xp, na, nb = _mix_ab_prompt(xp, *wts)
            pa.append(na)
            pb.append(nb)
            xs, na, nb = _mix_ab_sample(xs, step_major(state_conv_a[e]), step_major(state_conv_b[e]),
                                        *wts)
            sa.append(step_major(na))
            sb.append(step_major(nb))
        else:
            o = l // 2
            wts = (norm_g[l, 1], pool_w_h[o], pool_b[o], pool_scale[o])
            xp, npool = _pool_prompt(xp, *wts)
            pp.append(npool)
            xs, npool = _pool_sample(xs, step_major(state_pool[o]), PAST_LEN, *wts)
            sp.append(step_major(npool))
        xp = ffn_prompt(xp, l, 1, final_norm_g if last else None)
        xs = ffn_sample(xs, l, 1, final_norm_g if last else None)

    xs = step_major(xs.reshape(ts, bs, d))
    st = lambda parts: jnp.stack(parts, axis=0)
    return (xp, xs, st(pa), st(pb), st(pp), st(sa), st(sb), st(sp))
```

```python
import functools
from typing import Callable, NamedTuple

import jax
import jax.numpy as jnp
from jax.experimental import pallas as pl
from jax.experimental.pallas import tpu as pltpu

F32 = jnp.float32
BF16 = jnp.bfloat16

EPS = 1e-6
CONV_A_WIDTH = 31
CONV_B_WIDTH = 3
POOL_WINDOWS = (2, 4, 8, 16)
POOL_HIST = max(POOL_WINDOWS) - 1
PAST_LEN = 16384

SUBLANES = 8
BF16_SUBLANES = 16
LANES = 128
VMEM_LIMIT_BYTES = 56 * 1024 * 1024

FFN_TILE = 512
FFN_CHUNK = 256
CONV_ROWS = 64


def _rms(x, g):
    ms = jnp.mean(x * x, axis=-1, keepdims=True)
    return x * jax.lax.rsqrt(ms + EPS) * g


def _dot(a, b):
    return jnp.dot(a, b, preferred_element_type=F32)


def _const_spec(shape):
    nd = len(shape)
    return pl.BlockSpec(shape, lambda *_: (0,) * nd, pipeline_mode=pl.Buffered(1))


def _params(sem):
    return pltpu.CompilerParams(dimension_semantics=sem, vmem_limit_bytes=VMEM_LIMIT_BYTES)


def _cast_side(cast_src, cast_dst):
    for src, dst in zip(cast_src, cast_dst):
        dst[...] = src[...].astype(BF16)


def _ffn_tile(x_ref, g_ref, win_ref, wout_ref, fg_ref, *, d_ff, chunk):
    h = _rms(x_ref[...], g_ref[...]).astype(BF16)
    acc = None
    for c in range(d_ff // chunk):
        lo = c * chunk
        gate = _dot(h, win_ref[:, lo:lo + chunk])
        up = _dot(h, win_ref[:, d_ff + lo:d_ff + lo + chunk])
        a = (gate * jax.nn.sigmoid(gate) * up).astype(BF16)
        p = _dot(a, wout_ref[lo:lo + chunk, :])
        acc = p if acc is None else acc + p
    y = x_ref[...] + 0.5 * acc
    if fg_ref is not None:
        y = _rms(y, fg_ref[...])
    return y


class _Mixer(NamedTuple):
    name: str
    tile_fn: Callable
    ins: tuple
    state_shapes: tuple
    scratch: tuple
    seq_len: int


def _ffn_kernel(*refs, d_ff, chunk, final, n_cast, mixer, tiles_per_seq):
    refs = list(refs)
    take = lambda k: [refs.pop(0) for _ in range(k)]
    (x_ref,) = take(1)
    mix_in = take(len(mixer.ins)) if mixer else []
    g_ref, win_ref, wout_ref = take(3)
    fg_ref = take(1)[0] if final else None
    cast_src = take(n_cast)
    (o_ref,) = take(1)
    state = take(len(mixer.state_shapes)) if mixer else []
    cast_dst = take(n_cast)
    mix_scratch = take(len(mixer.scratch)) if mixer else []
    ffn = functools.partial(_ffn_tile, g_ref=g_ref, win_ref=win_ref, wout_ref=wout_ref, fg_ref=fg_ref,
                            d_ff=d_ff, chunk=chunk)
    _cast_side(cast_src, cast_dst)
    if not mixer:
        o_ref[...] = ffn(x_ref)
        return

    (xm,) = refs
    i = pl.program_id(0)
    xm[...] = mixer.tile_fn(x_ref, jax.lax.rem(i, tiles_per_seq), i == 0, mix_in, state, mix_scratch)
    o_ref[...] = ffn(xm)


def _cast_specs(w, sel, n_steps):
    r, c = w.shape[len(sel):]
    n_blocks = n_steps
    while r % n_blocks or (r // n_blocks) % BF16_SUBLANES:
        assert n_blocks % 2 == 0, (r, n_steps)
        n_blocks //= 2
    rep = n_steps // n_blocks
    rows = r // n_blocks
    blk = lambda i: jnp.minimum(i, n_steps - 1) // rep
    in_spec = pl.BlockSpec((None,) * len(sel) + (rows, c), lambda i: tuple(sel) + (blk(i), 0))
    out_spec = pl.BlockSpec((rows, c), lambda i: (blk(i), 0))
    return in_spec, out_spec, jax.ShapeDtypeStruct((r, c), BF16)


def _ffn(x, g, w_in, w_out, final_g=None, cast=(), mixer=None):
    n, d = x.shape
    d_ff = w_out.shape[0]
    tile = min(FFN_TILE, n)
    assert n % tile == 0 and d_ff % FFN_CHUNK == 0
    n_tiles = n // tile
    final = final_g is not None
    tiles_per_seq = 1
    if mixer:
        assert mixer.seq_len % tile == 0
        tiles_per_seq = mixer.seq_len // tile
    x_spec = o_spec = pl.BlockSpec((tile, d), lambda i: (i, 0))
    ins, specs = [x], [x_spec]
    if mixer:
        ins += list(mixer.ins)
        specs += [_const_spec(a.shape) for a in mixer.ins]
    ins += [g.reshape(1, d), w_in, w_out]
    specs += [_const_spec((1, d)), _const_spec(w_in.shape), _const_spec(w_out.shape)]
    if final:
        ins.append(final_g.reshape(1, d))
        specs.append(_const_spec((1, d)))
    out_shapes, out_specs = [jax.ShapeDtypeStruct((n, d), F32)], [o_spec]
    scratch = []
    if mixer:
        for shp in mixer.state_shapes:
            out_shapes.append(jax.ShapeDtypeStruct(shp, F32))
            out_specs.append(pl.BlockSpec((None,) + tuple(shp[1:]), lambda i: (i // tiles_per_seq, 0, 0)))
        scratch = list(mixer.scratch) + [pltpu.VMEM((tile, d), F32)]
    cast_ins, cast_specs_, cast_out_specs, cast_out_shapes = [], [], [], []
    for w, sel in cast:
        in_spec, out_spec, out_shape = _cast_specs(w, sel, n_tiles)
        cast_ins.append(w)
        cast_specs_.append(in_spec)
        cast_out_specs.append(out_spec)
        cast_out_shapes.append(out_shape)
    n_state = len(mixer.state_shapes) if mixer else 0
    outs = pl.pallas_call(
        functools.partial(_ffn_kernel, d_ff=d_ff, chunk=FFN_CHUNK, final=final, n_cast=len(cast),
                          mixer=mixer, tiles_per_seq=tiles_per_seq),
        out_shape=tuple(out_shapes + cast_out_shapes),
        grid=(n_tiles,),
        in_specs=specs + cast_specs_,
        out_specs=tuple(out_specs + cast_out_specs),
        scratch_shapes=scratch,
        compiler_params=_params(("arbitrary",)),
        name=(mixer.name + "_" if mixer else "") + ("ffn_final" if final else "ffn"),
    )(*ins, *cast_ins)
    return outs[0], tuple(outs[1:1 + n_state]), tuple(outs[1 + n_state:])


def _layernorm_silu(y, g, b):
    mu = jnp.mean(y, axis=-1, keepdims=True)
    yc = y - mu
    yn = yc * jax.lax.rsqrt(jnp.mean(yc * yc, axis=-1, keepdims=True) + EPS) * g + b
    return yn * jax.nn.sigmoid(yn)


def _split_z(z, d_a, d_b):
    a_val = z[:, :d_a]
    a_gate = z[:, d_a:2 * d_a]
    b_gate = z[:, 2 * d_a:2 * d_a + d_b]
    c_gate = z[:, 2 * d_a + d_b:2 * d_a + 2 * d_b]
    v = z[:, 2 * d_a + 2 * d_b:]
    return a_val, a_gate, b_gate, c_gate, v


A_PAD = 32
B_PAD = 8
SKEW_TAPS = CONV_A_WIDTH + SUBLANES - 1


def _mix_ab_tile(x_ref, t, first_step, in_refs, state_refs, scratch_refs, *, tile, tiles_per_seq, d_a, d_b):
    g_ref, win_ref, caw_ref, cab_ref, lng_ref, lnb_ref, cbw_ref, wout_ref = in_refs
    na_ref, nb_ref = state_refs
    exta, extb, ycat, wskew, bgate = scratch_refs

    @pl.when(first_step)
    def _():
        wskew[...] = jnp.zeros(wskew.shape, F32)
        for dd in range(SKEW_TAPS):
            for p in range(SUBLANES):
                if 0 <= dd - p < CONV_A_WIDTH:
                    wskew[dd, p:p + 1, :] = caw_ref[dd - p:dd - p + 1, :]

    @pl.when(t == 0)
    def _():
        exta[0:A_PAD, :] = jnp.zeros((A_PAD, d_a), F32)
        extb[0:B_PAD, :] = jnp.zeros((B_PAD, d_b), F32)

    h = _rms(x_ref[...], g_ref[...]).astype(BF16)
    z = _dot(h, win_ref[...])
    a_val, a_gate, b_gate, c_gate, v = _split_z(z, d_a, d_b)
    exta[A_PAD:A_PAD + tile, :] = a_val * jax.nn.sigmoid(a_gate)
    extb[B_PAD:B_PAD + tile, :] = c_gate * v
    bgate[...] = b_gate

    off_a = A_PAD - (CONV_A_WIDTH - 1)
    off_b = B_PAD - (CONV_B_WIDTH - 1)
    for r0 in range(0, tile, CONV_ROWS):
        rs = slice(r0, r0 + CONV_ROWS)
        for c0 in range(0, d_a, LANES):
            cs = slice(c0, c0 + LANES)
            n_acc = CONV_ROWS // SUBLANES
            accs = [jnp.broadcast_to(cab_ref[:, cs], (SUBLANES, LANES))] * n_acc
            for dd in range(SKEW_TAPS):
                wv = wskew[dd, :, cs]
                for m in range(n_acc):
                    src = r0 + SUBLANES * m + off_a + dd
                    accs[m] = accs[m] + wv * jnp.broadcast_to(exta[src:src + 1, cs], (SUBLANES, LANES))
            for m in range(n_acc):
                ycat[r0 + SUBLANES * m:r0 + SUBLANES * (m + 1), cs] = accs[m]
        ycat[rs, :d_a] = _layernorm_silu(ycat[rs, :d_a], lng_ref[...], lnb_ref[...])
        for c0 in range(0, d_b, LANES):
            cs = slice(c0, c0 + LANES)
            acc = None
            for k in range(CONV_B_WIDTH):
                term = cbw_ref[k:k + 1, cs] * extb[r0 + off_b + k:r0 + off_b + k + CONV_ROWS, cs]
                acc = term if acc is None else acc + term
            ycat[rs, d_a + c0:d_a + c0 + LANES] = bgate[rs, cs] * acc

    out = _dot(ycat[...].astype(BF16), wout_ref[...])

    exta[0:A_PAD, :] = exta[tile:tile + A_PAD, :]
    extb[0:B_PAD, :] = extb[tile:tile + B_PAD, :]

    @pl.when(t == tiles_per_seq - 1)
    def _():
        na_ref[...] = exta[off_a:A_PAD, :]
        nb_ref[...] = extb[off_b:B_PAD, :]

    return x_ref[...] + out


def _mix_ab_prompt(n_seq, seq_len, g, w_in, caw, cab, lng, lnb, cbw, w_out):
    d_a = caw.shape[1]
    d_b = cbw.shape[1]
    tile = min(FFN_TILE, seq_len)
    assert tile >= A_PAD and tile % CONV_ROWS == 0
    row = lambda a: a.reshape(1, -1)
    return _Mixer(
        name="mix_ab",
        tile_fn=functools.partial(_mix_ab_tile, tile=tile, tiles_per_seq=seq_len // tile, d_a=d_a, d_b=d_b),
        ins=(row(g), w_in, caw, row(cab), row(lng), row(lnb), cbw, w_out),
        state_shapes=((n_seq, CONV_A_WIDTH - 1, d_a), (n_seq, CONV_B_WIDTH - 1, d_b)),
        scratch=(pltpu.VMEM((A_PAD + tile, d_a), F32), pltpu.VMEM((B_PAD + tile, d_b), F32),
                 pltpu.VMEM((tile, d_a + d_b), F32), pltpu.VMEM((SKEW_TAPS, SUBLANES, d_a), F32),
                 pltpu.VMEM((tile, d_b), F32)),
        seq_len=seq_len)


def _mix_ab_sample_kernel(x_ref, ha_ref, hb_ref, g_ref, win_ref, caw_ref, cab_ref, lng_ref, lnb_ref,
                          cbw_ref, wout_ref, o_ref, na_ref, nb_ref, ua_s, ub_s, ycat,
                          *, n_seq, t_new, d_a, d_b):
    hist_a = CONV_A_WIDTH - 1
    hist_b = CONV_B_WIDTH - 1

    def rows(t):
        return slice(t * n_seq, (t + 1) * n_seq)

    def ext_a(j):
        return ha_ref[j] if j < hist_a else ua_s[rows(j - hist_a), :]

    def ext_b(j):
        return hb_ref[j] if j < hist_b else ub_s[rows(j - hist_b), :]

    x = x_ref[...]
    h = _rms(x, g_ref[...]).astype(BF16)
    z = _dot(h, win_ref[...])
    a_val, a_gate, b_gate, c_gate, v = _split_z(z, d_a, d_b)
    ua_s[...] = a_val * jax.nn.sigmoid(a_gate)
    ub_s[...] = c_gate * v

    for t in range(t_new):
        acc = jnp.broadcast_to(cab_ref[...], (n_seq, d_a))
        for k in range(CONV_A_WIDTH):
            acc = acc + caw_ref[k:k + 1, :] * ext_a(t + k)
        ycat[rows(t), 0:d_a] = _layernorm_silu(acc, lng_ref[...], lnb_ref[...])
        acc = None
        for k in range(CONV_B_WIDTH):
            term = cbw_ref[k:k + 1, :] * ext_b(t + k)
            acc = term if acc is None else acc + term
        ycat[rows(t), d_a:d_a + d_b] = acc

    ya = ycat[:, :d_a]
    yb = b_gate * ycat[:, d_a:]
    out = _dot(ya.astype(BF16), wout_ref[:d_a, :]) + _dot(yb.astype(BF16), wout_ref[d_a:, :])
    o_ref[...] = x + out

    for j in range(hist_a):
        na_ref[j] = ext_a(j + t_new)
    for j in range(hist_b):
        nb_ref[j] = ext_b(j + t_new)


def _mix_ab_sample(x, hist_a, hist_b, g, w_in, caw, cab, lng, lnb, cbw, w_out):
    n, d = x.shape
    b = hist_a.shape[1]
    d_a = caw.shape[1]
    d_b = cbw.shape[1]
    row = lambda a: a.reshape(1, -1)
    ins = [x, hist_a, hist_b, row(g), w_in, caw, row(cab), row(lng), row(lnb), cbw, w_out]
    return pl.pallas_call(
        functools.partial(_mix_ab_sample_kernel, n_seq=b, t_new=n // b, d_a=d_a, d_b=d_b),
        out_shape=(jax.ShapeDtypeStruct((n, d), F32),
                   jax.ShapeDtypeStruct(hist_a.shape, F32),
                   jax.ShapeDtypeStruct(hist_b.shape, F32)),
        scratch_shapes=[pltpu.VMEM((n, d_a), F32), pltpu.VMEM((n, d_b), F32),
                        pltpu.VMEM((n, d_a + d_b), F32)],
        compiler_params=pltpu.CompilerParams(vmem_limit_bytes=VMEM_LIMIT_BYTES),
        name="mix_ab_sample",
    )(*ins)


P_PAD = 16


def _pool_mix(pooled, pw_ref, pb_ref, ps_ref, group):
    outs = []
    for gi in range(len(POOL_WINDOWS)):
        cs = slice(gi * group, (gi + 1) * group)
        outs.append(_dot(pooled[:, cs].astype(BF16), pw_ref[gi]))
    mixed = jnp.concatenate(outs, axis=-1)
    return (mixed + pb_ref[...]) * ps_ref[...]


def _pool_tile(x_ref, t, first_step, in_refs, state_refs, scratch_refs, *, tile, tiles_per_seq, d, group):
    del first_step
    g_ref, pw_ref, pb_ref, ps_ref = in_refs
    (np_ref,) = state_refs
    exth, pooled_s = scratch_refs

    @pl.when(t == 0)
    def _():
        exth[0:P_PAD, :] = jnp.zeros((P_PAD, d), F32)

    exth[P_PAD:P_PAD + tile, :] = _rms(x_ref[...], g_ref[...])

    pos = (t * tile + jax.lax.broadcasted_iota(jnp.int32, (tile, group), 0)).astype(F32)
    for gi, w in enumerate(POOL_WINDOWS):
        assert w & (w - 1) == 0 and w <= P_PAD
        cs = slice(gi * group, (gi + 1) * group)
        s = exth[:, cs]
        shift = 1
        while shift < w:
            s = s + pltpu.roll(s, shift, 0)
            shift *= 2
        count = jnp.minimum(pos + 1.0, float(w))
        pooled_s[:, cs] = s[P_PAD:, :] / count - exth[P_PAD:P_PAD + tile, cs]

    y = x_ref[...] + _pool_mix(pooled_s[...], pw_ref, pb_ref, ps_ref, group)

    exth[0:P_PAD, :] = exth[tile:tile + P_PAD, :]

    @pl.when(t == tiles_per_seq - 1)
    def _():
        np_ref[...] = exth[P_PAD - POOL_HIST:P_PAD, :]

    return y


def _pool_prompt(n_seq, seq_len, d, g, pw, pb, ps):
    group = pw.shape[-1]
    tile = min(FFN_TILE, seq_len)
    assert tile >= P_PAD
    row = lambda a: a.reshape(1, -1)
    return _Mixer(
        name="pool",
        tile_fn=functools.partial(_pool_tile, tile=tile, tiles_per_seq=seq_len // tile, d=d, group=group),
        ins=(row(g), pw, row(pb), row(ps)),
        state_shapes=((n_seq, POOL_HIST, d),),
        scratch=(pltpu.VMEM((P_PAD + tile, d), F32), pltpu.VMEM((tile, d), F32)),
        seq_len=seq_len)


def _pool_sample_kernel(x_ref, hp_ref, g_ref, pw_ref, pb_ref, ps_ref, o_ref, np_ref, h_s, pooled_s,
                        *, n_seq, t_new, pos0, group):
    def rows(t):
        return slice(t * n_seq, (t + 1) * n_seq)

    def ext(j, cs):
        return hp_ref[j, :, cs] if j < POOL_HIST else h_s[rows(j - POOL_HIST), cs]

    x = x_ref[...]
    h_s[...] = _rms(x, g_ref[...])

    for t in range(t_new):
        for gi, w in enumerate(POOL_WINDOWS):
            cs = slice(gi * group, (gi + 1) * group)
            s = ext(POOL_HIST + t, cs)
            for j in range(1, w):
                s = s + ext(POOL_HIST + t - j, cs)
            count = float(min(pos0 + t + 1, w))
            pooled_s[rows(t), cs] = s / count - ext(POOL_HIST + t, cs)

    o_ref[...] = x + _pool_mix(pooled_s[...], pw_ref, pb_ref, ps_ref, group)

    full = slice(None)
    for j in range(POOL_HIST):
        np_ref[j] = ext(j + t_new, full)


def _pool_sample(x, hist, pos0, g, pw, pb, ps):
    n, d = x.shape
    b = hist.shape[1]
    group = pw.shape[-1]
    row = lambda a: a.reshape(1, -1)
    return pl.pallas_call(
        functools.partial(_pool_sample_kernel, n_seq=b, t_new=n // b, pos0=pos0, group=group),
        out_shape=(jax.ShapeDtypeStruct((n, d), F32), jax.ShapeDtypeStruct(hist.shape, F32)),
        scratch_shapes=[pltpu.VMEM((n, d), F32), pltpu.VMEM((n, d), F32)],
        compiler_params=pltpu.CompilerParams(vmem_limit_bytes=VMEM_LIMIT_BYTES),
        name="pool_sample",
    )(x, hist, row(g), pw, row(pb), row(ps))


def kernel(x_prompt, x_sample, state_conv_a, state_conv_b, state_pool, norm_g, ffn_w_in, ffn_w_out,
           mix_w_in, conv_a_w, conv_a_b, ln_a_g, ln_a_b, conv_b_w, mix_w_out, pool_w, pool_b,
           pool_scale, final_norm_g):
    depth = norm_g.shape[0]
    d = x_prompt.shape[-1]
    bs, ts, _ = x_sample.shape
    step_major = lambda a: jnp.swapaxes(a, 0, 1)

    n_even, n_odd = mix_w_in.shape[0], pool_w.shape[0]
    n_groups, group = pool_w.shape[1], pool_w.shape[2]
    pool_w2 = pool_w.reshape(n_odd, n_groups * group, group)
    mixer_casts = ([(mix_w_in, (e,)) for e in range(n_even)] + [(mix_w_out, (e,)) for e in range(n_even)]
                   + [(pool_w2, (o,)) for o in range(n_odd)])
    ffn_w = {(0, 0): (ffn_w_in[0, 0].astype(BF16), ffn_w_out[0, 0].astype(BF16))}
    mix_w_in_h, mix_w_out_h, pool_w_h = [], [], []

    def ffn_prompt(x, l, which, final_g=None, mixer=None):
        nxt = (l, 1) if which == 0 else (l + 1, 0)
        cast = [(ffn_w_in, nxt), (ffn_w_out, nxt)] if nxt[0] < depth else []
        if (l, which) == (0, 0):
            cast = cast + mixer_casts
        y, states, casts = _ffn(x, norm_g[l, 2 * which], *ffn_w[(l, which)], final_g, cast, mixer)
        if nxt[0] < depth:
            ffn_w[nxt] = casts[:2]
        if (l, which) == (0, 0):
            rest = casts[len(casts) - len(mixer_casts):]
            mix_w_in_h.extend(rest[:n_even])
            mix_w_out_h.extend(rest[n_even:2 * n_even])
            pool_w_h.extend(w.reshape(n_groups, group, group) for w in rest[2 * n_even:])
        return y, states

    def ffn_sample(x, l, which, final_g=None):
        return _ffn(x, norm_g[l, 2 * which], *ffn_w[(l, which)], final_g)[0]

    bp, tp, _ = x_prompt.shape
    xp, xs = x_prompt.reshape(bp * tp, d), step_major(x_sample).reshape(ts * bs, d)
    pa, pb, pp, sa, sb, sp = [], [], [], [], [], []
    for l in range(depth):
        final_g = final_norm_g if l == depth - 1 else None
        xp, _ = ffn_prompt(xp, l, 0)
        xs = ffn_sample(xs, l, 0)
        if l % 2 == 0:
            e = l // 2
            wts = (norm_g[l, 1], mix_w_in_h[e], conv_a_w[e], conv_a_b[e], ln_a_g[e], ln_a_b[e],
                   conv_b_w[e], mix_w_out_h[e])
            xp, (na, nb) = ffn_prompt(xp, l, 1, final_g, _mix_ab_prompt(bp, tp, *wts))
            pa.append(na)
            pb.append(nb)
            xs, na, nb = _mix_ab_sample(xs, step_major(state_conv_a[e]), step_major(state_conv_b[e]),
                                        *wts)
            sa.append(step_major(na))
            sb.append(step_major(nb))
        else:
            o = l // 2
            wts = (norm_g[l, 1], pool_w_h[o], pool_b[o], pool_scale[o])
            xp, (npool,) = ffn_prompt(xp, l, 1, final_g, _pool_prompt(bp, tp, d, *wts))
            pp.append(npool)
            xs, npool = _pool_sample(xs, step_major(state_pool[o]), PAST_LEN, *wts)
            sp.append(step_major(npool))
        xs = ffn_sample(xs, l, 1, final_g)

    xp = xp.reshape(bp, tp, d)
    xs = step_major(xs.reshape(ts, bs, d))
    st = lambda parts: jnp.stack(parts, axis=0)
    return (xp, xs, st(pa), st(pb), st(pp), st(sa), st(sb), st(sp))
```

```python
import functools
from typing import Callable, NamedTuple

import jax
import jax.numpy as jnp
from jax.experimental import pallas as pl
from jax.experimental.pallas import tpu as pltpu

F32 = jnp.float32
BF16 = jnp.bfloat16

EPS = 1e-6
CONV_A_WIDTH = 31
CONV_B_WIDTH = 3
POOL_WINDOWS = (2, 4, 8, 16)
POOL_HIST = max(POOL_WINDOWS) - 1
PAST_LEN = 16384

SUBLANES = 8
BF16_SUBLANES = 16
LANES = 128
VMEM_LIMIT_BYTES = 56 * 1024 * 1024

FFN_TILE = 512
PLAIN_FFN_TILE = 1024
FFN_CHUNK = 256
CONV_ROWS = 64


def _rms(x, g):
    ms = jnp.mean(x * x, axis=-1, keepdims=True)
    return x * jax.lax.rsqrt(ms + EPS) * g


def _dot(a, b):
    return jnp.dot(a, b, preferred_element_type=F32)


def _const_spec(shape):
    nd = len(shape)
    return pl.BlockSpec(shape, lambda *_: (0,) * nd, pipeline_mode=pl.Buffered(1))


def _params(sem):
    return pltpu.CompilerParams(dimension_semantics=sem, vmem_limit_bytes=VMEM_LIMIT_BYTES)


def _cast_side(cast_src, cast_dst):
    for src, dst in zip(cast_src, cast_dst):
        dst[...] = src[...].astype(BF16)


def _ffn_tile(x_ref, g_ref, win_ref, wout_ref, fg_ref, *, d_ff, chunk):
    h = _rms(x_ref[...], g_ref[...]).astype(BF16)
    acc = None
    for c in range(d_ff // chunk):
        lo = c * chunk
        gate = _dot(h, win_ref[:, lo:lo + chunk])
        up = _dot(h, win_ref[:, d_ff + lo:d_ff + lo + chunk])
        a = (gate * jax.nn.sigmoid(gate) * up).astype(BF16)
        p = _dot(a, wout_ref[lo:lo + chunk, :])
        acc = p if acc is None else acc + p
    y = x_ref[...] + 0.5 * acc
    if fg_ref is not None:
        y = _rms(y, fg_ref[...])
    return y


class _Mixer(NamedTuple):
    name: str
    tile_fn: Callable
    ins: tuple
    state_shapes: tuple
    scratch: tuple
    seq_len: int


def _ffn_kernel(*refs, d_ff, chunk, final, n_cast, mixer, tiles_per_seq, n_tiles, has_extra):
    refs = list(refs)
    take = lambda k: [refs.pop(0) for _ in range(k)]
    (x_ref,) = take(1)
    extra_ref = take(1)[0] if has_extra else None
    mix_in = take(len(mixer.ins)) if mixer else []
    g_ref, win_ref, wout_ref = take(3)
    fg_ref = take(1)[0] if final else None
    cast_src = take(n_cast)
    (o_ref,) = take(1)
    extra_o_ref = take(1)[0] if has_extra else None
    state = take(len(mixer.state_shapes)) if mixer else []
    cast_dst = take(n_cast)
    mix_scratch = take(len(mixer.scratch)) if mixer else []
    ffn = functools.partial(_ffn_tile, g_ref=g_ref, win_ref=win_ref, wout_ref=wout_ref, fg_ref=fg_ref,
                            d_ff=d_ff, chunk=chunk)
    i = pl.program_id(0)
    _cast_side(cast_src, cast_dst)

    def main_step():
        if mixer:
            (xm,) = refs
            xm[...] = mixer.tile_fn(x_ref, jax.lax.rem(i, tiles_per_seq), i == 0, mix_in, state, mix_scratch)
            o_ref[...] = ffn(xm)
        else:
            o_ref[...] = ffn(x_ref)

    if not has_extra:
        main_step()
        return

    pl.when(i < n_tiles)(main_step)

    @pl.when(i == n_tiles)
    def _():
        extra_o_ref[...] = ffn(extra_ref)


def _cast_specs(w, sel, n_steps):
    r, c = w.shape[len(sel):]
    n_blocks = n_steps
    while r % n_blocks or (r // n_blocks) % BF16_SUBLANES:
        assert n_blocks % 2 == 0, (r, n_steps)
        n_blocks //= 2
    rep = n_steps // n_blocks
    rows = r // n_blocks
    blk = lambda i: jnp.minimum(i, n_steps - 1) // rep
    in_spec = pl.BlockSpec((None,) * len(sel) + (rows, c), lambda i: tuple(sel) + (blk(i), 0))
    out_spec = pl.BlockSpec((rows, c), lambda i: (blk(i), 0))
    return in_spec, out_spec, jax.ShapeDtypeStruct((r, c), BF16)


def _ffn(x, g, w_in, w_out, final_g=None, cast=(), mixer=None, extra=None, tile=FFN_TILE):
    n, d = x.shape
    d_ff = w_out.shape[0]
    tile = min(tile, n)
    assert n % tile == 0 and d_ff % FFN_CHUNK == 0
    n_tiles = n // tile
    final = final_g is not None
    has_extra = extra is not None
    tiles_per_seq = 1
    if mixer:
        assert mixer.seq_len % tile == 0
        tiles_per_seq = mixer.seq_len // tile
    main_tile = lambda i: jnp.minimum(i, n_tiles - 1)
    x_spec = o_spec = pl.BlockSpec((tile, d), lambda i: (main_tile(i), 0))
    ins, specs = [x], [x_spec]
    out_shapes, out_specs = [jax.ShapeDtypeStruct((n, d), F32)], [o_spec]
    if has_extra:
        ins.append(extra)
        specs.append(pl.BlockSpec(extra.shape, lambda i: (0, 0)))
        out_shapes.append(jax.ShapeDtypeStruct(extra.shape, F32))
        out_specs.append(pl.BlockSpec(extra.shape, lambda i: (0, 0)))
    if mixer:
        ins += list(mixer.ins)
        specs += [_const_spec(a.shape) for a in mixer.ins]
    ins += [g.reshape(1, d), w_in, w_out]
    specs += [_const_spec((1, d)), _const_spec(w_in.shape), _const_spec(w_out.shape)]
    if final:
        ins.append(final_g.reshape(1, d))
        specs.append(_const_spec((1, d)))
    scratch = []
    if mixer:
        for shp in mixer.state_shapes:
            out_shapes.append(jax.ShapeDtypeStruct(shp, F32))
            out_specs.append(pl.BlockSpec((None,) + tuple(shp[1:]),
                                          lambda i: (main_tile(i) // tiles_per_seq, 0, 0)))
        scratch = list(mixer.scratch) + [pltpu.VMEM((tile, d), F32)]
    cast_ins, cast_specs_, cast_out_specs, cast_out_shapes = [], [], [], []
    for w, sel in cast:
        in_spec, out_spec, out_shape = _cast_specs(w, sel, n_tiles)
        cast_ins.append(w)
        cast_specs_.append(in_spec)
        cast_out_specs.append(out_spec)
        cast_out_shapes.append(out_shape)
    n_state = len(mixer.state_shapes) if mixer else 0
    outs = list(pl.pallas_call(
        functools.partial(_ffn_kernel, d_ff=d_ff, chunk=FFN_CHUNK, final=final, n_cast=len(cast),
                          mixer=mixer, tiles_per_seq=tiles_per_seq, n_tiles=n_tiles, has_extra=has_extra),
        out_shape=tuple(out_shapes + cast_out_shapes),
        grid=(n_tiles + (1 if has_extra else 0),),
        in_specs=specs + cast_specs_,
        out_specs=tuple(out_specs + cast_out_specs),
        scratch_shapes=scratch,
        compiler_params=_params(("arbitrary",)),
        name=(mixer.name + "_" if mixer else "") + ("ffn_final" if final else "ffn"),
    )(*ins, *cast_ins))
    y = outs.pop(0)
    y_extra = outs.pop(0) if has_extra else None
    return y, y_extra, tuple(outs[:n_state]), tuple(outs[n_state:])


def _layernorm_silu(y, g, b):
    mu = jnp.mean(y, axis=-1, keepdims=True)
    yc = y - mu
    yn = yc * jax.lax.rsqrt(jnp.mean(yc * yc, axis=-1, keepdims=True) + EPS) * g + b
    return yn * jax.nn.sigmoid(yn)


def _split_z(z, d_a, d_b):
    a_val = z[:, :d_a]
    a_gate = z[:, d_a:2 * d_a]
    b_gate = z[:, 2 * d_a:2 * d_a + d_b]
    c_gate = z[:, 2 * d_a + d_b:2 * d_a + 2 * d_b]
    v = z[:, 2 * d_a + 2 * d_b:]
    return a_val, a_gate, b_gate, c_gate, v


A_PAD = 32
B_PAD = 8
SKEW_TAPS = CONV_A_WIDTH + SUBLANES - 1


def _mix_ab_tile(x_ref, t, first_step, in_refs, state_refs, scratch_refs, *, tile, tiles_per_seq, d_a, d_b):
    g_ref, win_ref, caw_ref, cab_ref, lng_ref, lnb_ref, cbw_ref, wout_ref = in_refs
    na_ref, nb_ref = state_refs
    exta, extb, ycat, wskew, bgate = scratch_refs

    @pl.when(first_step)
    def _():
        wskew[...] = jnp.zeros(wskew.shape, F32)
        for dd in range(SKEW_TAPS):
            for p in range(SUBLANES):
                if 0 <= dd - p < CONV_A_WIDTH:
                    wskew[dd, p:p + 1, :] = caw_ref[dd - p:dd - p + 1, :]

    @pl.when(t == 0)
    def _():
        exta[0:A_PAD, :] = jnp.zeros((A_PAD, d_a), F32)
        extb[0:B_PAD, :] = jnp.zeros((B_PAD, d_b), F32)

    h = _rms(x_ref[...], g_ref[...]).astype(BF16)
    z = _dot(h, win_ref[...])
    a_val, a_gate, b_gate, c_gate, v = _split_z(z, d_a, d_b)
    exta[A_PAD:A_PAD + tile, :] = a_val * jax.nn.sigmoid(a_gate)
    extb[B_PAD:B_PAD + tile, :] = c_gate * v
    bgate[...] = b_gate

    off_a = A_PAD - (CONV_A_WIDTH - 1)
    off_b = B_PAD - (CONV_B_WIDTH - 1)
    for r0 in range(0, tile, CONV_ROWS):
        rs = slice(r0, r0 + CONV_ROWS)
        for c0 in range(0, d_a, LANES):
            cs = slice(c0, c0 + LANES)
            n_acc = CONV_ROWS // SUBLANES
            accs = [jnp.broadcast_to(cab_ref[:, cs], (SUBLANES, LANES))] * n_acc
            for dd in range(SKEW_TAPS):
                wv = wskew[dd, :, cs]
                for m in range(n_acc):
                    src = r0 + SUBLANES * m + off_a + dd
                    accs[m] = accs[m] + wv * jnp.broadcast_to(exta[src:src + 1, cs], (SUBLANES, LANES))
            for m in range(n_acc):
                ycat[r0 + SUBLANES * m:r0 + SUBLANES * (m + 1), cs] = accs[m]
        ycat[rs, :d_a] = _layernorm_silu(ycat[rs, :d_a], lng_ref[...], lnb_ref[...])
        for c0 in range(0, d_b, LANES):
            cs = slice(c0, c0 + LANES)
            acc = None
            for k in range(CONV_B_WIDTH):
                term = cbw_ref[k:k + 1, cs] * extb[r0 + off_b + k:r0 + off_b + k + CONV_ROWS, cs]
                acc = term if acc is None else acc + term
            ycat[rs, d_a + c0:d_a + c0 + LANES] = bgate[rs, cs] * acc

    out = _dot(ycat[...].astype(BF16), wout_ref[...])

    exta[0:A_PAD, :] = exta[tile:tile + A_PAD, :]
    extb[0:B_PAD, :] = extb[tile:tile + B_PAD, :]

    @pl.when(t == tiles_per_seq - 1)
    def _():
        na_ref[...] = exta[off_a:A_PAD, :]
        nb_ref[...] = extb[off_b:B_PAD, :]

    return x_ref[...] + out


def _mix_ab_prompt(n_seq, seq_len, g, w_in, caw, cab, lng, lnb, cbw, w_out):
    d_a = caw.shape[1]
    d_b = cbw.shape[1]
    tile = min(FFN_TILE, seq_len)
    assert tile >= A_PAD and tile % CONV_ROWS == 0
    row = lambda a: a.reshape(1, -1)
    return _Mixer(
        name="mix_ab",
        tile_fn=functools.partial(_mix_ab_tile, tile=tile, tiles_per_seq=seq_len // tile, d_a=d_a, d_b=d_b),
        ins=(row(g), w_in, caw, row(cab), row(lng), row(lnb), cbw, w_out),
        state_shapes=((n_seq, CONV_A_WIDTH - 1, d_a), (n_seq, CONV_B_WIDTH - 1, d_b)),
        scratch=(pltpu.VMEM((A_PAD + tile, d_a), F32), pltpu.VMEM((B_PAD + tile, d_b), F32),
                 pltpu.VMEM((tile, d_a + d_b), F32), pltpu.VMEM((SKEW_TAPS, SUBLANES, d_a), F32),
                 pltpu.VMEM((tile, d_b), F32)),
        seq_len=seq_len)


def _mix_ab_sample_kernel(x_ref, ha_ref, hb_ref, g_ref, win_ref, caw_ref, cab_ref, lng_ref, lnb_ref,
                          cbw_ref, wout_ref, o_ref, na_ref, nb_ref, ua_s, ub_s, ycat,
                          *, n_seq, t_new, d_a, d_b):
    hist_a = CONV_A_WIDTH - 1
    hist_b = CONV_B_WIDTH - 1

    def rows(t):
        return slice(t * n_seq, (t + 1) * n_seq)

    def ext_a(j):
        return ha_ref[j] if j < hist_a else ua_s[rows(j - hist_a), :]

    def ext_b(j):
        return hb_ref[j] if j < hist_b else ub_s[rows(j - hist_b), :]

    x = x_ref[...]
    h = _rms(x, g_ref[...]).astype(BF16)
    z = _dot(h, win_ref[...])
    a_val, a_gate, b_gate, c_gate, v = _split_z(z, d_a, d_b)
    ua_s[...] = a_val * jax.nn.sigmoid(a_gate)
    ub_s[...] = c_gate * v

    for t in range(t_new):
        acc = jnp.broadcast_to(cab_ref[...], (n_seq, d_a))
        for k in range(CONV_A_WIDTH):
            acc = acc + caw_ref[k:k + 1, :] * ext_a(t + k)
        ycat[rows(t), 0:d_a] = _layernorm_silu(acc, lng_ref[...], lnb_ref[...])
        acc = None
        for k in range(CONV_B_WIDTH):
            term = cbw_ref[k:k + 1, :] * ext_b(t + k)
            acc = term if acc is None else acc + term
        ycat[rows(t), d_a:d_a + d_b] = acc

    ya = ycat[:, :d_a]
    yb = b_gate * ycat[:, d_a:]
    out = _dot(ya.astype(BF16), wout_ref[:d_a, :]) + _dot(yb.astype(BF16), wout_ref[d_a:, :])
    o_ref[...] = x + out

    for j in range(hist_a):
        na_ref[j] = ext_a(j + t_new)
    for j in range(hist_b):
        nb_ref[j] = ext_b(j + t_new)


def _mix_ab_sample(x, hist_a, hist_b, g, w_in, caw, cab, lng, lnb, cbw, w_out):
    n, d = x.shape
    b = hist_a.shape[1]
    d_a = caw.shape[1]
    d_b = cbw.shape[1]
    row = lambda a: a.reshape(1, -1)
    ins = [x, hist_a, hist_b, row(g), w_in, caw, row(cab), row(lng), row(lnb), cbw, w_out]
    return pl.pallas_call(
        functools.partial(_mix_ab_sample_kernel, n_seq=b, t_new=n // b, d_a=d_a, d_b=d_b),
        out_shape=(jax.ShapeDtypeStruct((n, d), F32),
                   jax.ShapeDtypeStruct(hist_a.shape, F32),
                   jax.ShapeDtypeStruct(hist_b.shape, F32)),
        scratch_shapes=[pltpu.VMEM((n, d_a), F32), pltpu.VMEM((n, d_b), F32),
                        pltpu.VMEM((n, d_a + d_b), F32)],
        compiler_params=pltpu.CompilerParams(vmem_limit_bytes=VMEM_LIMIT_BYTES),
        name="mix_ab_sample",
    )(*ins)


P_PAD = 16


def _pool_mix(pooled, pw_ref, pb_ref, ps_ref, group):
    outs = []
    for gi in range(len(POOL_WINDOWS)):
        cs = slice(gi * group, (gi + 1) * group)
        outs.append(_dot(pooled[:, cs].astype(BF16), pw_ref[gi]))
    mixed = jnp.concatenate(outs, axis=-1)
    return (mixed + pb_ref[...]) * ps_ref[...]


def _pool_tile(x_ref, t, first_step, in_refs, state_refs, scratch_refs, *, tile, tiles_per_seq, d, group):
    del first_step
    g_ref, pw_ref, pb_ref, ps_ref = in_refs
    (np_ref,) = state_refs
    exth, pooled_s = scratch_refs

    @pl.when(t == 0)
    def _():
        exth[0:P_PAD, :] = jnp.zeros((P_PAD, d), F32)

    exth[P_PAD:P_PAD + tile, :] = _rms(x_ref[...], g_ref[...])

    pos = (t * tile + jax.lax.broadcasted_iota(jnp.int32, (tile, group), 0)).astype(F32)
    for gi, w in enumerate(POOL_WINDOWS):
        assert w & (w - 1) == 0 and w <= P_PAD
        cs = slice(gi * group, (gi + 1) * group)
        s = exth[:, cs]
        shift = 1
        while shift < w:
            s = s + pltpu.roll(s, shift, 0)
            shift *= 2
        count = jnp.minimum(pos + 1.0, float(w))
        pooled_s[:, cs] = s[P_PAD:, :] / count - exth[P_PAD:P_PAD + tile, cs]

    y = x_ref[...] + _pool_mix(pooled_s[...], pw_ref, pb_ref, ps_ref, group)

    exth[0:P_PAD, :] = exth[tile:tile + P_PAD, :]

    @pl.when(t == tiles_per_seq - 1)
    def _():
        np_ref[...] = exth[P_PAD - POOL_HIST:P_PAD, :]

    return y


def _pool_prompt(n_seq, seq_len, d, g, pw, pb, ps):
    group = pw.shape[-1]
    tile = min(FFN_TILE, seq_len)
    assert tile >= P_PAD
    row = lambda a: a.reshape(1, -1)
    return _Mixer(
        name="pool",
        tile_fn=functools.partial(_pool_tile, tile=tile, tiles_per_seq=seq_len // tile, d=d, group=group),
        ins=(row(g), pw, row(pb), row(ps)),
        state_shapes=((n_seq, POOL_HIST, d),),
        scratch=(pltpu.VMEM((P_PAD + tile, d), F32), pltpu.VMEM((tile, d), F32)),
        seq_len=seq_len)


def _pool_sample_kernel(x_ref, hp_ref, g_ref, pw_ref, pb_ref, ps_ref, o_ref, np_ref, h_s, pooled_s,
                        *, n_seq, t_new, pos0, group):
    def rows(t):
        return slice(t * n_seq, (t + 1) * n_seq)

    def ext(j, cs):
        return hp_ref[j, :, cs] if j < POOL_HIST else h_s[rows(j - POOL_HIST), cs]

    x = x_ref[...]
    h_s[...] = _rms(x, g_ref[...])

    for t in range(t_new):
        for gi, w in enumerate(POOL_WINDOWS):
            cs = slice(gi * group, (gi + 1) * group)
            s = ext(POOL_HIST + t, cs)
            for j in range(1, w):
                s = s + ext(POOL_HIST + t - j, cs)
            count = float(min(pos0 + t + 1, w))
            pooled_s[rows(t), cs] = s / count - ext(POOL_HIST + t, cs)

    o_ref[...] = x + _pool_mix(pooled_s[...], pw_ref, pb_ref, ps_ref, group)

    full = slice(None)
    for j in range(POOL_HIST):
        np_ref[j] = ext(j + t_new, full)


def _pool_sample(x, hist, pos0, g, pw, pb, ps):
    n, d = x.shape
    b = hist.shape[1]
    group = pw.shape[-1]
    row = lambda a: a.reshape(1, -1)
    return pl.pallas_call(
        functools.partial(_pool_sample_kernel, n_seq=b, t_new=n // b, pos0=pos0, group=group),
        out_shape=(jax.ShapeDtypeStruct((n, d), F32), jax.ShapeDtypeStruct(hist.shape, F32)),
        scratch_shapes=[pltpu.VMEM((n, d), F32), pltpu.VMEM((n, d), F32)],
        compiler_params=pltpu.CompilerParams(vmem_limit_bytes=VMEM_LIMIT_BYTES),
        name="pool_sample",
    )(x, hist, row(g), pw, row(pb), row(ps))


def kernel(x_prompt, x_sample, state_conv_a, state_conv_b, state_pool, norm_g, ffn_w_in, ffn_w_out,
           mix_w_in, conv_a_w, conv_a_b, ln_a_g, ln_a_b, conv_b_w, mix_w_out, pool_w, pool_b,
           pool_scale, final_norm_g):
    depth = norm_g.shape[0]
    d = x_prompt.shape[-1]
    bs, ts, _ = x_sample.shape
    step_major = lambda a: jnp.swapaxes(a, 0, 1)

    n_even, n_odd = mix_w_in.shape[0], pool_w.shape[0]
    n_groups, group = pool_w.shape[1], pool_w.shape[2]
    pool_w2 = pool_w.reshape(n_odd, n_groups * group, group)
    mixer_casts = ([(mix_w_in, (e,)) for e in range(n_even)] + [(mix_w_out, (e,)) for e in range(n_even)]
                   + [(pool_w2, (o,)) for o in range(n_odd)])
    ffn_w = {(0, 0): (ffn_w_in[0, 0].astype(BF16), ffn_w_out[0, 0].astype(BF16))}
    mix_w_in_h, mix_w_out_h, pool_w_h = [], [], []

    def ffn(xp, xs, l, which, final_g=None, mixer=None):
        nxt = (l, 1) if which == 0 else (l + 1, 0)
        cast = [(ffn_w_in, nxt), (ffn_w_out, nxt)] if nxt[0] < depth else []
        if (l, which) == (0, 0):
            cast = cast + mixer_casts
        yp, ys, states, casts = _ffn(xp, norm_g[l, 2 * which], *ffn_w[(l, which)], final_g, cast, mixer,
                                     extra=xs, tile=FFN_TILE if mixer else PLAIN_FFN_TILE)
        if nxt[0] < depth:
            ffn_w[nxt] = casts[:2]
        if (l, which) == (0, 0):
            rest = casts[len(casts) - len(mixer_casts):]
            mix_w_in_h.extend(rest[:n_even])
            mix_w_out_h.extend(rest[n_even:2 * n_even])
            pool_w_h.extend(w.reshape(n_groups, group, group) for w in rest[2 * n_even:])
        return yp, ys, states

    bp, tp, _ = x_prompt.shape
    xp, xs = x_prompt.reshape(bp * tp, d), step_major(x_sample).reshape(ts * bs, d)
    pa, pb, pp, sa, sb, sp = [], [], [], [], [], []
    for l in range(depth):
        final_g = final_norm_g if l == depth - 1 else None
        xp, xs, _ = ffn(xp, xs, l, 0)
        if l % 2 == 0:
            e = l // 2
            wts = (norm_g[l, 1], mix_w_in_h[e], conv_a_w[e], conv_a_b[e], ln_a_g[e], ln_a_b[e],
                   conv_b_w[e], mix_w_out_h[e])
            xs, na, nb = _mix_ab_sample(xs, step_major(state_conv_a[e]), step_major(state_conv_b[e]),
                                        *wts)
            sa.append(step_major(na))
            sb.append(step_major(nb))
            xp, xs, (na, nb) = ffn(xp, xs, l, 1, final_g, _mix_ab_prompt(bp, tp, *wts))
            pa.append(na)
            pb.append(nb)
        else:
            o = l // 2
            wts = (norm_g[l, 1], pool_w_h[o], pool_b[o], pool_scale[o])
            xs, npool = _pool_sample(xs, step_major(state_pool[o]), PAST_LEN, *wts)
            sp.append(step_major(npool))
            xp, xs, (npool,) = ffn(xp, xs, l, 1, final_g, _pool_prompt(bp, tp, d, *wts))
            pp.append(npool)

    xp = xp.reshape(bp, tp, d)
    xs = step_major(xs.reshape(ts, bs, d))
    st = lambda parts: jnp.stack(parts, axis=0)
    return (xp, xs, st(pa), st(pb), st(pp), st(sa), st(sb), st(sp))
```

```python
import functools
from typing import Callable, NamedTuple

import jax
import jax.numpy as jnp
from jax.experimental import pallas as pl
from jax.experimental.pallas import tpu as pltpu

F32 = jnp.float32
BF16 = jnp.bfloat16

EPS = 1e-6
CONV_A_WIDTH = 31
CONV_B_WIDTH = 3
POOL_WINDOWS = (2, 4, 8, 16)
POOL_HIST = max(POOL_WINDOWS) - 1
PAST_LEN = 16384

SUBLANES = 8
BF16_SUBLANES = 16
LANES = 128
VMEM_LIMIT_BYTES = 56 * 1024 * 1024

FFN_TILE = 512
PLAIN_FFN_TILE = 1024
FFN_CHUNK = 256
CONV_ROWS = 32


def _rms(x, g):
    ms = jnp.mean(x * x, axis=-1, keepdims=True)
    return x * jax.lax.rsqrt(ms + EPS) * g


def _dot(a, b):
    return jnp.dot(a, b, preferred_element_type=F32)


def _const_spec(shape):
    nd = len(shape)
    return pl.BlockSpec(shape, lambda *_: (0,) * nd, pipeline_mode=pl.Buffered(1))


def _params(sem):
    return pltpu.CompilerParams(dimension_semantics=sem, vmem_limit_bytes=VMEM_LIMIT_BYTES)


def _cast_side(cast_src, cast_dst):
    for src, dst in zip(cast_src, cast_dst):
        dst[...] = src[...].astype(BF16)


def _ffn_tile(x_ref, g_ref, win_ref, wout_ref, fg_ref, *, d_ff, chunk):
    h = _rms(x_ref[...], g_ref[...]).astype(BF16)
    acc = None
    for c in range(d_ff // chunk):
        lo = c * chunk
        gate = _dot(h, win_ref[:, lo:lo + chunk])
        up = _dot(h, win_ref[:, d_ff + lo:d_ff + lo + chunk])
        a = (gate * jax.nn.sigmoid(gate) * up).astype(BF16)
        p = _dot(a, wout_ref[lo:lo + chunk, :])
        acc = p if acc is None else acc + p
    y = x_ref[...] + 0.5 * acc
    if fg_ref is not None:
        y = _rms(y, fg_ref[...])
    return y


class _Mixer(NamedTuple):
    name: str
    tile_fn: Callable
    ins: tuple
    state_shapes: tuple
    scratch: tuple
    seq_len: int


def _ffn_kernel(*refs, d_ff, chunk, final, n_cast, mixer, tiles_per_seq, n_tiles, has_extra):
    refs = list(refs)
    take = lambda k: [refs.pop(0) for _ in range(k)]
    (x_ref,) = take(1)
    extra_ref = take(1)[0] if has_extra else None
    mix_in = take(len(mixer.ins)) if mixer else []
    g_ref, win_ref, wout_ref = take(3)
    fg_ref = take(1)[0] if final else None
    cast_src = take(n_cast)
    (o_ref,) = take(1)
    extra_o_ref = take(1)[0] if has_extra else None
    state = take(len(mixer.state_shapes)) if mixer else []
    cast_dst = take(n_cast)
    mix_scratch = take(len(mixer.scratch)) if mixer else []
    ffn = functools.partial(_ffn_tile, g_ref=g_ref, win_ref=win_ref, wout_ref=wout_ref, fg_ref=fg_ref,
                            d_ff=d_ff, chunk=chunk)
    i = pl.program_id(0)
    _cast_side(cast_src, cast_dst)

    def main_step():
        if mixer:
            (xm,) = refs
            xm[...] = mixer.tile_fn(x_ref, jax.lax.rem(i, tiles_per_seq), i == 0, mix_in, state, mix_scratch)
            o_ref[...] = ffn(xm)
        else:
            o_ref[...] = ffn(x_ref)

    if not has_extra:
        main_step()
        return

    pl.when(i < n_tiles)(main_step)

    @pl.when(i == n_tiles)
    def _():
        extra_o_ref[...] = ffn(extra_ref)


def _cast_specs(w, sel, n_steps):
    r, c = w.shape[len(sel):]
    n_blocks = n_steps
    while r % n_blocks or (r // n_blocks) % BF16_SUBLANES:
        assert n_blocks % 2 == 0, (r, n_steps)
        n_blocks //= 2
    rep = n_steps // n_blocks
    rows = r // n_blocks
    blk = lambda i: jnp.minimum(i, n_steps - 1) // rep
    in_spec = pl.BlockSpec((None,) * len(sel) + (rows, c), lambda i: tuple(sel) + (blk(i), 0))
    out_spec = pl.BlockSpec((rows, c), lambda i: (blk(i), 0))
    return in_spec, out_spec, jax.ShapeDtypeStruct((r, c), BF16)


def _ffn(x, g, w_in, w_out, final_g=None, cast=(), mixer=None, extra=None, tile=FFN_TILE):
    n, d = x.shape
    d_ff = w_out.shape[0]
    tile = min(tile, n)
    assert n % tile == 0 and d_ff % FFN_CHUNK == 0
    n_tiles = n // tile
    final = final_g is not None
    has_extra = extra is not None
    tiles_per_seq = 1
    if mixer:
        assert mixer.seq_len % tile == 0
        tiles_per_seq = mixer.seq_len // tile
    main_tile = lambda i: jnp.minimum(i, n_tiles - 1)
    x_spec = o_spec = pl.BlockSpec((tile, d), lambda i: (main_tile(i), 0))
    ins, specs = [x], [x_spec]
    out_shapes, out_specs = [jax.ShapeDtypeStruct((n, d), F32)], [o_spec]
    if has_extra:
        ins.append(extra)
        specs.append(pl.BlockSpec(extra.shape, lambda i: (0, 0)))
        out_shapes.append(jax.ShapeDtypeStruct(extra.shape, F32))
        out_specs.append(pl.BlockSpec(extra.shape, lambda i: (0, 0)))
    if mixer:
        ins += list(mixer.ins)
        specs += [_const_spec(a.shape) for a in mixer.ins]
    ins += [g.reshape(1, d), w_in, w_out]
    specs += [_const_spec((1, d)), _const_spec(w_in.shape), _const_spec(w_out.shape)]
    if final:
        ins.append(final_g.reshape(1, d))
        specs.append(_const_spec((1, d)))
    scratch = []
    if mixer:
        for shp in mixer.state_shapes:
            out_shapes.append(jax.ShapeDtypeStruct(shp, F32))
            out_specs.append(pl.BlockSpec((None,) + tuple(shp[1:]),
                                          lambda i: (main_tile(i) // tiles_per_seq, 0, 0)))
        scratch = list(mixer.scratch) + [pltpu.VMEM((tile, d), F32)]
    cast_ins, cast_specs_, cast_out_specs, cast_out_shapes = [], [], [], []
    for w, sel in cast:
        in_spec, out_spec, out_shape = _cast_specs(w, sel, n_tiles)
        cast_ins.append(w)
        cast_specs_.append(in_spec)
        cast_out_specs.append(out_spec)
        cast_out_shapes.append(out_shape)
    n_state = len(mixer.state_shapes) if mixer else 0
    outs = list(pl.pallas_call(
        functools.partial(_ffn_kernel, d_ff=d_ff, chunk=FFN_CHUNK, final=final, n_cast=len(cast),
                          mixer=mixer, tiles_per_seq=tiles_per_seq, n_tiles=n_tiles, has_extra=has_extra),
        out_shape=tuple(out_shapes + cast_out_shapes),
        grid=(n_tiles + (1 if has_extra else 0),),
        in_specs=specs + cast_specs_,
        out_specs=tuple(out_specs + cast_out_specs),
        scratch_shapes=scratch,
        compiler_params=_params(("arbitrary",)),
        name=(mixer.name + "_" if mixer else "") + ("ffn_final" if final else "ffn"),
    )(*ins, *cast_ins))
    y = outs.pop(0)
    y_extra = outs.pop(0) if has_extra else None
    return y, y_extra, tuple(outs[:n_state]), tuple(outs[n_state:])


def _layernorm_silu(y, g, b):
    mu = jnp.mean(y, axis=-1, keepdims=True)
    yc = y - mu
    yn = yc * jax.lax.rsqrt(jnp.mean(yc * yc, axis=-1, keepdims=True) + EPS) * g + b
    return yn * jax.nn.sigmoid(yn)


def _split_z(z, d_a, d_b):
    a_val = z[:, :d_a]
    a_gate = z[:, d_a:2 * d_a]
    b_gate = z[:, 2 * d_a:2 * d_a + d_b]
    c_gate = z[:, 2 * d_a + d_b:2 * d_a + 2 * d_b]
    v = z[:, 2 * d_a + 2 * d_b:]
    return a_val, a_gate, b_gate, c_gate, v


A_PAD = 32
B_PAD = 8
SKEW_TAPS = CONV_A_WIDTH + SUBLANES - 1


def _mix_ab_tile(x_ref, t, first_step, in_refs, state_refs, scratch_refs, *, tile, tiles_per_seq, d_a, d_b):
    g_ref, win_ref, caw_ref, cab_ref, lng_ref, lnb_ref, cbw_ref, wout_ref = in_refs
    na_ref, nb_ref = state_refs
    exta, extb, ycat, wskew, bgate = scratch_refs

    @pl.when(first_step)
    def _():
        wskew[...] = jnp.zeros(wskew.shape, F32)
        for dd in range(SKEW_TAPS):
            for p in range(SUBLANES):
                if 0 <= dd - p < CONV_A_WIDTH:
                    wskew[dd, p:p + 1, :] = caw_ref[dd - p:dd - p + 1, :]

    @pl.when(t == 0)
    def _():
        exta[0:A_PAD, :] = jnp.zeros((A_PAD, d_a), F32)
        extb[0:B_PAD, :] = jnp.zeros((B_PAD, d_b), F32)

    h = _rms(x_ref[...], g_ref[...]).astype(BF16)
    z = _dot(h, win_ref[...])
    a_val, a_gate, b_gate, c_gate, v = _split_z(z, d_a, d_b)
    exta[A_PAD:A_PAD + tile, :] = a_val * jax.nn.sigmoid(a_gate)
    extb[B_PAD:B_PAD + tile, :] = c_gate * v
    bgate[...] = b_gate

    off_a = A_PAD - (CONV_A_WIDTH - 1)
    off_b = B_PAD - (CONV_B_WIDTH - 1)
    for r0 in range(0, tile, CONV_ROWS):
        rs = slice(r0, r0 + CONV_ROWS)
        for c0 in range(0, d_a, LANES):
            cs = slice(c0, c0 + LANES)
            n_acc = CONV_ROWS // SUBLANES
            accs = [jnp.broadcast_to(cab_ref[:, cs], (SUBLANES, LANES))] * n_acc
            for dd in range(SKEW_TAPS):
                wv = wskew[dd, :, cs]
                for m in range(n_acc):
                    src = r0 + SUBLANES * m + off_a + dd
                    accs[m] = accs[m] + wv * jnp.broadcast_to(exta[src:src + 1, cs], (SUBLANES, LANES))
            for m in range(n_acc):
                ycat[r0 + SUBLANES * m:r0 + SUBLANES * (m + 1), cs] = accs[m]
        ycat[rs, :d_a] = _layernorm_silu(ycat[rs, :d_a], lng_ref[...], lnb_ref[...])
        for c0 in range(0, d_b, LANES):
            cs = slice(c0, c0 + LANES)
            acc = None
            for k in range(CONV_B_WIDTH):
                term = cbw_ref[k:k + 1, cs] * extb[r0 + off_b + k:r0 + off_b + k + CONV_ROWS, cs]
                acc = term if acc is None else acc + term
            ycat[rs, d_a + c0:d_a + c0 + LANES] = bgate[rs, cs] * acc

    out = _dot(ycat[...].astype(BF16), wout_ref[...])

    exta[0:A_PAD, :] = exta[tile:tile + A_PAD, :]
    extb[0:B_PAD, :] = extb[tile:tile + B_PAD, :]

    @pl.when(t == tiles_per_seq - 1)
    def _():
        na_ref[...] = exta[off_a:A_PAD, :]
        nb_ref[...] = extb[off_b:B_PAD, :]

    return x_ref[...] + out


def _mix_ab_prompt(n_seq, seq_len, g, w_in, caw, cab, lng, lnb, cbw, w_out):
    d_a = caw.shape[1]
    d_b = cbw.shape[1]
    tile = min(FFN_TILE, seq_len)
    assert tile >= A_PAD and tile % CONV_ROWS == 0
    row = lambda a: a.reshape(1, -1)
    return _Mixer(
        name="mix_ab",
        tile_fn=functools.partial(_mix_ab_tile, tile=tile, tiles_per_seq=seq_len // tile, d_a=d_a, d_b=d_b),
        ins=(row(g), w_in, caw, row(cab), row(lng), row(lnb), cbw, w_out),
        state_shapes=((n_seq, CONV_A_WIDTH - 1, d_a), (n_seq, CONV_B_WIDTH - 1, d_b)),
        scratch=(pltpu.VMEM((A_PAD + tile, d_a), F32), pltpu.VMEM((B_PAD + tile, d_b), F32),
                 pltpu.VMEM((tile, d_a + d_b), F32), pltpu.VMEM((SKEW_TAPS, SUBLANES, d_a), F32),
                 pltpu.VMEM((tile, d_b), F32)),
        seq_len=seq_len)


def _mix_ab_sample_kernel(x_ref, ha_ref, hb_ref, g_ref, win_ref, caw_ref, cab_ref, lng_ref, lnb_ref,
                          cbw_ref, wout_ref, o_ref, na_ref, nb_ref, ua_s, ub_s, ycat, ha_s, hb_s,
                          *, n_seq, t_new, d_a, d_b):
    hist_a = CONV_A_WIDTH - 1
    hist_b = CONV_B_WIDTH - 1
    for j in range(hist_a):
        ha_s[j] = ha_ref[:, j, :]
    for j in range(hist_b):
        hb_s[j] = hb_ref[:, j, :]

    def rows(t):
        return slice(t * n_seq, (t + 1) * n_seq)

    def ext_a(j):
        return ha_s[j] if j < hist_a else ua_s[rows(j - hist_a), :]

    def ext_b(j):
        return hb_s[j] if j < hist_b else ub_s[rows(j - hist_b), :]

    x = x_ref[...]
    h = _rms(x, g_ref[...]).astype(BF16)
    z = _dot(h, win_ref[...])
    a_val, a_gate, b_gate, c_gate, v = _split_z(z, d_a, d_b)
    ua_s[...] = a_val * jax.nn.sigmoid(a_gate)
    ub_s[...] = c_gate * v

    for t in range(t_new):
        acc = jnp.broadcast_to(cab_ref[...], (n_seq, d_a))
        for k in range(CONV_A_WIDTH):
            acc = acc + caw_ref[k:k + 1, :] * ext_a(t + k)
        ycat[rows(t), 0:d_a] = _layernorm_silu(acc, lng_ref[...], lnb_ref[...])
        acc = None
        for k in range(CONV_B_WIDTH):
            term = cbw_ref[k:k + 1, :] * ext_b(t + k)
            acc = term if acc is None else acc + term
        ycat[rows(t), d_a:d_a + d_b] = acc

    ya = ycat[:, :d_a]
    yb = b_gate * ycat[:, d_a:]
    out = _dot(ya.astype(BF16), wout_ref[:d_a, :]) + _dot(yb.astype(BF16), wout_ref[d_a:, :])
    o_ref[...] = x + out

    for j in range(hist_a):
        na_ref[:, j, :] = ext_a(j + t_new)
    for j in range(hist_b):
        nb_ref[:, j, :] = ext_b(j + t_new)


def _mix_ab_sample(x, hist_a, hist_b, g, w_in, caw, cab, lng, lnb, cbw, w_out):
    n, d = x.shape
    b = hist_a.shape[0]
    d_a = caw.shape[1]
    d_b = cbw.shape[1]
    row = lambda a: a.reshape(1, -1)
    ins = [x, hist_a, hist_b, row(g), w_in, caw, row(cab), row(lng), row(lnb), cbw, w_out]
    return pl.pallas_call(
        functools.partial(_mix_ab_sample_kernel, n_seq=b, t_new=n // b, d_a=d_a, d_b=d_b),
        out_shape=(jax.ShapeDtypeStruct((n, d), F32),
                   jax.ShapeDtypeStruct(hist_a.shape, F32),
                   jax.ShapeDtypeStruct(hist_b.shape, F32)),
        scratch_shapes=[pltpu.VMEM((n, d_a), F32), pltpu.VMEM((n, d_b), F32),
                        pltpu.VMEM((n, d_a + d_b), F32),
                        pltpu.VMEM((CONV_A_WIDTH - 1, b, d_a), F32),
                        pltpu.VMEM((CONV_B_WIDTH - 1, b, d_b), F32)],
        compiler_params=pltpu.CompilerParams(vmem_limit_bytes=VMEM_LIMIT_BYTES),
        name="mix_ab_sample",
    )(*ins)


P_PAD = 16


def _pool_mix(pooled, pw_ref, pb_ref, ps_ref, group):
    outs = []
    for gi in range(len(POOL_WINDOWS)):
        cs = slice(gi * group, (gi + 1) * group)
        outs.append(_dot(pooled[:, cs].astype(BF16), pw_ref[gi]))
    mixed = jnp.concatenate(outs, axis=-1)
    return (mixed + pb_ref[...]) * ps_ref[...]


def _pool_tile(x_ref, t, first_step, in_refs, state_refs, scratch_refs, *, tile, tiles_per_seq, d, group):
    del first_step
    g_ref, pw_ref, pb_ref, ps_ref = in_refs
    (np_ref,) = state_refs
    exth, pooled_s = scratch_refs

    @pl.when(t == 0)
    def _():
        exth[0:P_PAD, :] = jnp.zeros((P_PAD, d), F32)

    exth[P_PAD:P_PAD + tile, :] = _rms(x_ref[...], g_ref[...])

    pos = (t * tile + jax.lax.broadcasted_iota(jnp.int32, (tile, group), 0)).astype(F32)
    for gi, w in enumerate(POOL_WINDOWS):
        assert w & (w - 1) == 0 and w <= P_PAD
        cs = slice(gi * group, (gi + 1) * group)
        s = exth[:, cs]
        shift = 1
        while shift < w:
            s = s + pltpu.roll(s, shift, 0)
            shift *= 2
        count = jnp.minimum(pos + 1.0, float(w))
        pooled_s[:, cs] = s[P_PAD:, :] / count - exth[P_PAD:P_PAD + tile, cs]

    y = x_ref[...] + _pool_mix(pooled_s[...], pw_ref, pb_ref, ps_ref, group)

    exth[0:P_PAD, :] = exth[tile:tile + P_PAD, :]

    @pl.when(t == tiles_per_seq - 1)
    def _():
        np_ref[...] = exth[P_PAD - POOL_HIST:P_PAD, :]

    return y


def _pool_prompt(n_seq, seq_len, d, g, pw, pb, ps):
    group = pw.shape[-1]
    tile = min(FFN_TILE, seq_len)
    assert tile >= P_PAD
    row = lambda a: a.reshape(1, -1)
    return _Mixer(
        name="pool",
        tile_fn=functools.partial(_pool_tile, tile=tile, tiles_per_seq=seq_len // tile, d=d, group=group),
        ins=(row(g), pw, row(pb), row(ps)),
        state_shapes=((n_seq, POOL_HIST, d),),
        scratch=(pltpu.VMEM((P_PAD + tile, d), F32), pltpu.VMEM((tile, d), F32)),
        seq_len=seq_len)


def _pool_sample_kernel(x_ref, hp_ref, g_ref, pw_ref, pb_ref, ps_ref, o_ref, np_ref, h_s, pooled_s,
                        hp_s, *, n_seq, t_new, pos0, group):
    for j in range(POOL_HIST):
        hp_s[j] = hp_ref[:, j, :]

    def rows(t):
        return slice(t * n_seq, (t + 1) * n_seq)

    def ext(j, cs):
        return hp_s[j, :, cs] if j < POOL_HIST else h_s[rows(j - POOL_HIST), cs]

    x = x_ref[...]
    h_s[...] = _rms(x, g_ref[...])

    for t in range(t_new):
        for gi, w in enumerate(POOL_WINDOWS):
            cs = slice(gi * group, (gi + 1) * group)
            s = ext(POOL_HIST + t, cs)
            for j in range(1, w):
                s = s + ext(POOL_HIST + t - j, cs)
            count = float(min(pos0 + t + 1, w))
            pooled_s[rows(t), cs] = s / count - ext(POOL_HIST + t, cs)

    o_ref[...] = x + _pool_mix(pooled_s[...], pw_ref, pb_ref, ps_ref, group)

    full = slice(None)
    for j in range(POOL_HIST):
        np_ref[:, j, :] = ext(j + t_new, full)


def _pool_sample(x, hist, pos0, g, pw, pb, ps):
    n, d = x.shape
    b = hist.shape[0]
    group = pw.shape[-1]
    row = lambda a: a.reshape(1, -1)
    return pl.pallas_call(
        functools.partial(_pool_sample_kernel, n_seq=b, t_new=n // b, pos0=pos0, group=group),
        out_shape=(jax.ShapeDtypeStruct((n, d), F32), jax.ShapeDtypeStruct(hist.shape, F32)),
        scratch_shapes=[pltpu.VMEM((n, d), F32), pltpu.VMEM((n, d), F32),
                        pltpu.VMEM((POOL_HIST, b, d), F32)],
        compiler_params=pltpu.CompilerParams(vmem_limit_bytes=VMEM_LIMIT_BYTES),
        name="pool_sample",
    )(x, hist, row(g), pw, row(pb), row(ps))


def kernel(x_prompt, x_sample, state_conv_a, state_conv_b, state_pool, norm_g, ffn_w_in, ffn_w_out,
           mix_w_in, conv_a_w, conv_a_b, ln_a_g, ln_a_b, conv_b_w, mix_w_out, pool_w, pool_b,
           pool_scale, final_norm_g):
    depth = norm_g.shape[0]
    d = x_prompt.shape[-1]
    bs, ts, _ = x_sample.shape
    step_major = lambda a: jnp.swapaxes(a, 0, 1)

    n_even, n_odd = mix_w_in.shape[0], pool_w.shape[0]
    n_groups, group = pool_w.shape[1], pool_w.shape[2]
    pool_w2 = pool_w.reshape(n_odd, n_groups * group, group)
    mixer_casts = ([(mix_w_in, (e,)) for e in range(n_even)] + [(mix_w_out, (e,)) for e in range(n_even)]
                   + [(pool_w2, (o,)) for o in range(n_odd)])
    ffn_w = {(0, 0): (ffn_w_in[0, 0].astype(BF16), ffn_w_out[0, 0].astype(BF16))}
    mix_w_in_h, mix_w_out_h, pool_w_h = [], [], []

    def ffn(xp, xs, l, which, final_g=None, mixer=None):
        nxt = (l, 1) if which == 0 else (l + 1, 0)
        cast = [(ffn_w_in, nxt), (ffn_w_out, nxt)] if nxt[0] < depth else []
        if (l, which) == (0, 0):
            cast = cast + mixer_casts
        yp, ys, states, casts = _ffn(xp, norm_g[l, 2 * which], *ffn_w[(l, which)], final_g, cast, mixer,
                                     extra=xs, tile=FFN_TILE if mixer else PLAIN_FFN_TILE)
        if nxt[0] < depth:
            ffn_w[nxt] = casts[:2]
        if (l, which) == (0, 0):
            rest = casts[len(casts) - len(mixer_casts):]
            mix_w_in_h.extend(rest[:n_even])
            mix_w_out_h.extend(rest[n_even:2 * n_even])
            pool_w_h.extend(w.reshape(n_groups, group, group) for w in rest[2 * n_even:])
        return yp, ys, states

    bp, tp, _ = x_prompt.shape
    xp, xs = x_prompt.reshape(bp * tp, d), step_major(x_sample).reshape(ts * bs, d)
    pa, pb, pp, sa, sb, sp = [], [], [], [], [], []
    for l in range(depth):
        final_g = final_norm_g if l == depth - 1 else None
        xp, xs, _ = ffn(xp, xs, l, 0)
        if l % 2 == 0:
            e = l // 2
            wts = (norm_g[l, 1], mix_w_in_h[e], conv_a_w[e], conv_a_b[e], ln_a_g[e], ln_a_b[e],
                   conv_b_w[e], mix_w_out_h[e])
            xs, na, nb = _mix_ab_sample(xs, state_conv_a[e], state_conv_b[e], *wts)
            sa.append(na)
            sb.append(nb)
            xp, xs, (na, nb) = ffn(xp, xs, l, 1, final_g, _mix_ab_prompt(bp, tp, *wts))
            pa.append(na)
            pb.append(nb)
        else:
            o = l // 2
            wts = (norm_g[l, 1], pool_w_h[o], pool_b[o], pool_scale[o])
            xs, npool = _pool_sample(xs, state_pool[o], PAST_LEN, *wts)
            sp.append(npool)
            xp, xs, (npool,) = ffn(xp, xs, l, 1, final_g, _pool_prompt(bp, tp, d, *wts))
            pp.append(npool)

    xp = xp.reshape(bp, tp, d)
    xs = step_major(xs.reshape(ts, bs, d))
    st = lambda parts: jnp.stack(parts, axis=0)
    return (xp, xs, st(pa), st(pb), st(pp), st(sa), st(sb), st(sp))
```

```python
import functools
from typing import Callable, NamedTuple

import jax
import jax.numpy as jnp
from jax.experimental import pallas as pl
from jax.experimental.pallas import tpu as pltpu

F32 = jnp.float32
BF16 = jnp.bfloat16

EPS = 1e-6
CONV_A_WIDTH = 31
CONV_B_WIDTH = 3
POOL_WINDOWS = (2, 4, 8, 16)
POOL_HIST = max(POOL_WINDOWS) - 1
PAST_LEN = 16384

SUBLANES = 8
BF16_SUBLANES = 16
LANES = 128
VMEM_LIMIT_BYTES = 56 * 1024 * 1024

FFN_TILE = 512
PLAIN_FFN_TILE = 1024
FFN_CHUNK = 256
CONV_ROWS = 32


def _rms(x, g):
    ms = jnp.mean(x * x, axis=-1, keepdims=True)
    return x * jax.lax.rsqrt(ms + EPS) * g


def _dot(a, b):
    return jnp.dot(a, b, preferred_element_type=F32)


def _const_spec(shape):
    nd = len(shape)
    return pl.BlockSpec(shape, lambda *_: (0,) * nd, pipeline_mode=pl.Buffered(1))


def _params(sem):
    return pltpu.CompilerParams(dimension_semantics=sem, vmem_limit_bytes=VMEM_LIMIT_BYTES)


def _cast_side(cast_src, cast_dst):
    for src, dst in zip(cast_src, cast_dst):
        dst[...] = src[...].astype(BF16)


def _ffn_tile(x_ref, g_ref, win_ref, wout_ref, fg_ref, *, d_ff, chunk):
    h = _rms(x_ref[...], g_ref[...]).astype(BF16)
    acc = None
    for c in range(d_ff // chunk):
        lo = c * chunk
        gate = _dot(h, win_ref[:, lo:lo + chunk])
        up = _dot(h, win_ref[:, d_ff + lo:d_ff + lo + chunk])
        a = (gate * jax.nn.sigmoid(gate) * up).astype(BF16)
        p = _dot(a, wout_ref[lo:lo + chunk, :])
        acc = p if acc is None else acc + p
    y = x_ref[...] + 0.5 * acc
    if fg_ref is not None:
        y = _rms(y, fg_ref[...])
    return y


class _Mixer(NamedTuple):
    name: str
    tile_fn: Callable
    ins: tuple
    state_shapes: tuple
    scratch: tuple
    seq_len: int


def _ffn_kernel(*refs, d_ff, chunk, final, n_cast, mixer, tiles_per_seq, n_tiles, has_extra):
    refs = list(refs)
    take = lambda k: [refs.pop(0) for _ in range(k)]
    (x_ref,) = take(1)
    extra_ref = take(1)[0] if has_extra else None
    mix_in = take(len(mixer.ins)) if mixer else []
    g_ref, win_ref, wout_ref = take(3)
    fg_ref = take(1)[0] if final else None
    cast_src = take(n_cast)
    (o_ref,) = take(1)
    extra_o_ref = take(1)[0] if has_extra else None
    state = take(len(mixer.state_shapes)) if mixer else []
    cast_dst = take(n_cast)
    mix_scratch = take(len(mixer.scratch)) if mixer else []
    ffn = functools.partial(_ffn_tile, g_ref=g_ref, win_ref=win_ref, wout_ref=wout_ref, fg_ref=fg_ref,
                            d_ff=d_ff, chunk=chunk)
    i = pl.program_id(0)
    _cast_side(cast_src, cast_dst)

    def main_step():
        if mixer:
            (xm,) = refs
            xm[...] = mixer.tile_fn(x_ref, jax.lax.rem(i, tiles_per_seq), i == 0, mix_in, state, mix_scratch)
            o_ref[...] = ffn(xm)
        else:
            o_ref[...] = ffn(x_ref)

    if not has_extra:
        main_step()
        return

    pl.when(i < n_tiles)(main_step)

    @pl.when(i == n_tiles)
    def _():
        extra_o_ref[...] = ffn(extra_ref)


def _cast_specs(w, sel, n_steps):
    r, c = w.shape[len(sel):]
    n_blocks = n_steps
    while r % n_blocks or (r // n_blocks) % BF16_SUBLANES:
        assert n_blocks % 2 == 0, (r, n_steps)
        n_blocks //= 2
    rep = n_steps // n_blocks
    rows = r // n_blocks
    blk = lambda i: jnp.minimum(i, n_steps - 1) // rep
    in_spec = pl.BlockSpec((None,) * len(sel) + (rows, c), lambda i: tuple(sel) + (blk(i), 0))
    out_spec = pl.BlockSpec((rows, c), lambda i: (blk(i), 0))
    return in_spec, out_spec, jax.ShapeDtypeStruct((r, c), BF16)


def _ffn(x, g, w_in, w_out, final_g=None, cast=(), mixer=None, extra=None, tile=FFN_TILE):
    n, d = x.shape
    d_ff = w_out.shape[0]
    tile = min(tile, n)
    assert n % tile == 0 and d_ff % FFN_CHUNK == 0
    n_tiles = n // tile
    final = final_g is not None
    has_extra = extra is not None
    tiles_per_seq = 1
    if mixer:
        assert mixer.seq_len % tile == 0
        tiles_per_seq = mixer.seq_len // tile
    main_tile = lambda i: jnp.minimum(i, n_tiles - 1)
    x_spec = o_spec = pl.BlockSpec((tile, d), lambda i: (main_tile(i), 0))
    ins, specs = [x], [x_spec]
    out_shapes, out_specs = [jax.ShapeDtypeStruct((n, d), F32)], [o_spec]
    if has_extra:
        ins.append(extra)
        specs.append(pl.BlockSpec(extra.shape, lambda i: (0, 0)))
        out_shapes.append(jax.ShapeDtypeStruct(extra.shape, F32))
        out_specs.append(pl.BlockSpec(extra.shape, lambda i: (0, 0)))
    if mixer:
        ins += list(mixer.ins)
        specs += [_const_spec(a.shape) for a in mixer.ins]
    ins += [g.reshape(1, d), w_in, w_out]
    specs += [_const_spec((1, d)), _const_spec(w_in.shape), _const_spec(w_out.shape)]
    if final:
        ins.append(final_g.reshape(1, d))
        specs.append(_const_spec((1, d)))
    scratch = []
    if mixer:
        for shp in mixer.state_shapes:
            out_shapes.append(jax.ShapeDtypeStruct(shp, F32))
            out_specs.append(pl.BlockSpec((None,) + tuple(shp[1:]),
                                          lambda i: (main_tile(i) // tiles_per_seq, 0, 0)))
        scratch = list(mixer.scratch) + [pltpu.VMEM((tile, d), F32)]
    cast_ins, cast_specs_, cast_out_specs, cast_out_shapes = [], [], [], []
    for w, sel in cast:
        in_spec, out_spec, out_shape = _cast_specs(w, sel, n_tiles)
        cast_ins.append(w)
        cast_specs_.append(in_spec)
        cast_out_specs.append(out_spec)
        cast_out_shapes.append(out_shape)
    n_state = len(mixer.state_shapes) if mixer else 0
    outs = list(pl.pallas_call(
        functools.partial(_ffn_kernel, d_ff=d_ff, chunk=FFN_CHUNK, final=final, n_cast=len(cast),
                          mixer=mixer, tiles_per_seq=tiles_per_seq, n_tiles=n_tiles, has_extra=has_extra),
        out_shape=tuple(out_shapes + cast_out_shapes),
        grid=(n_tiles + (1 if has_extra else 0),),
        in_specs=specs + cast_specs_,
        out_specs=tuple(out_specs + cast_out_specs),
        scratch_shapes=scratch,
        compiler_params=_params(("arbitrary",)),
        name=(mixer.name + "_" if mixer else "") + ("ffn_final" if final else "ffn"),
    )(*ins, *cast_ins))
    y = outs.pop(0)
    y_extra = outs.pop(0) if has_extra else None
    return y, y_extra, tuple(outs[:n_state]), tuple(outs[n_state:])


def _layernorm_silu(y, g, b):
    mu = jnp.mean(y, axis=-1, keepdims=True)
    yc = y - mu
    yn = yc * jax.lax.rsqrt(jnp.mean(yc * yc, axis=-1, keepdims=True) + EPS) * g + b
    return yn * jax.nn.sigmoid(yn)


def _split_z(z, d_a, d_b):
    a_val = z[:, :d_a]
    a_gate = z[:, d_a:2 * d_a]
    b_gate = z[:, 2 * d_a:2 * d_a + d_b]
    c_gate = z[:, 2 * d_a + d_b:2 * d_a + 2 * d_b]
    v = z[:, 2 * d_a + 2 * d_b:]
    return a_val, a_gate, b_gate, c_gate, v


A_PAD = 32
B_PAD = 8
SKEW_TAPS = CONV_A_WIDTH + SUBLANES - 1


def _mix_ab_tile(x_ref, t, first_step, in_refs, state_refs, scratch_refs, *, tile, tiles_per_seq, d_a, d_b):
    g_ref, win_ref, caw_ref, cab_ref, lng_ref, lnb_ref, cbw_ref, wout_ref = in_refs
    na_ref, nb_ref = state_refs
    exta, extb, ycat, wskew, bgate = scratch_refs

    @pl.when(first_step)
    def _():
        wskew[...] = jnp.zeros(wskew.shape, F32)
        for dd in range(SKEW_TAPS):
            for p in range(SUBLANES):
                if 0 <= dd - p < CONV_A_WIDTH:
                    wskew[dd, p:p + 1, :] = caw_ref[dd - p:dd - p + 1, :]

    @pl.when(t == 0)
    def _():
        exta[0:A_PAD, :] = jnp.zeros((A_PAD, d_a), F32)
        extb[0:B_PAD, :] = jnp.zeros((B_PAD, d_b), F32)

    h = _rms(x_ref[...], g_ref[...]).astype(BF16)
    z = _dot(h, win_ref[...])
    a_val, a_gate, b_gate, c_gate, v = _split_z(z, d_a, d_b)
    exta[A_PAD:A_PAD + tile, :] = a_val * jax.nn.sigmoid(a_gate)
    extb[B_PAD:B_PAD + tile, :] = c_gate * v
    bgate[...] = b_gate

    off_a = A_PAD - (CONV_A_WIDTH - 1)
    off_b = B_PAD - (CONV_B_WIDTH - 1)
    for r0 in range(0, tile, CONV_ROWS):
        rs = slice(r0, r0 + CONV_ROWS)
        for c0 in range(0, d_a, LANES):
            cs = slice(c0, c0 + LANES)
            n_acc = CONV_ROWS // SUBLANES
            accs = [jnp.broadcast_to(cab_ref[:, cs], (SUBLANES, LANES))] * n_acc
            for dd in range(SKEW_TAPS):
                wv = wskew[dd, :, cs]
                for m in range(n_acc):
                    src = r0 + SUBLANES * m + off_a + dd
                    accs[m] = accs[m] + wv * jnp.broadcast_to(exta[src:src + 1, cs], (SUBLANES, LANES))
            for m in range(n_acc):
                ycat[r0 + SUBLANES * m:r0 + SUBLANES * (m + 1), cs] = accs[m]
        ycat[rs, :d_a] = _layernorm_silu(ycat[rs, :d_a], lng_ref[...], lnb_ref[...])
        for c0 in range(0, d_b, LANES):
            cs = slice(c0, c0 + LANES)
            acc = None
            for k in range(CONV_B_WIDTH):
                term = cbw_ref[k:k + 1, cs] * extb[r0 + off_b + k:r0 + off_b + k + CONV_ROWS, cs]
                acc = term if acc is None else acc + term
            ycat[rs, d_a + c0:d_a + c0 + LANES] = bgate[rs, cs] * acc

    out = _dot(ycat[...].astype(BF16), wout_ref[...])

    exta[0:A_PAD, :] = exta[tile:tile + A_PAD, :]
    extb[0:B_PAD, :] = extb[tile:tile + B_PAD, :]

    @pl.when(t == tiles_per_seq - 1)
    def _():
        na_ref[...] = exta[off_a:A_PAD, :]
        nb_ref[...] = extb[off_b:B_PAD, :]

    return x_ref[...] + out


def _mix_ab_prompt(n_seq, seq_len, g, w_in, caw, cab, lng, lnb, cbw, w_out):
    d_a = caw.shape[1]
    d_b = cbw.shape[1]
    tile = min(FFN_TILE, seq_len)
    assert tile >= A_PAD and tile % CONV_ROWS == 0
    row = lambda a: a.reshape(1, -1)
    return _Mixer(
        name="mix_ab",
        tile_fn=functools.partial(_mix_ab_tile, tile=tile, tiles_per_seq=seq_len // tile, d_a=d_a, d_b=d_b),
        ins=(row(g), w_in, caw, row(cab), row(lng), row(lnb), cbw, w_out),
        state_shapes=((n_seq, CONV_A_WIDTH - 1, d_a), (n_seq, CONV_B_WIDTH - 1, d_b)),
        scratch=(pltpu.VMEM((A_PAD + tile, d_a), F32), pltpu.VMEM((B_PAD + tile, d_b), F32),
                 pltpu.VMEM((tile, d_a + d_b), F32), pltpu.VMEM((SKEW_TAPS, SUBLANES, d_a), F32),
                 pltpu.VMEM((tile, d_b), F32)),
        seq_len=seq_len)


def _mix_ab_sample_kernel(x_ref, ha_ref, hb_ref, g_ref, win_ref, caw_ref, cab_ref, lng_ref, lnb_ref,
                          cbw_ref, wout_ref, o_ref, na_ref, nb_ref, ua_s, ub_s, ycat,
                          *, n_seq, t_new, d_a, d_b):
    hist_a = CONV_A_WIDTH - 1
    hist_b = CONV_B_WIDTH - 1

    def rows(t):
        return slice(t * n_seq, (t + 1) * n_seq)

    def ext_a(j):
        return ha_ref[j] if j < hist_a else ua_s[rows(j - hist_a), :]

    def ext_b(j):
        return hb_ref[j] if j < hist_b else ub_s[rows(j - hist_b), :]

    x = x_ref[...]
    h = _rms(x, g_ref[...]).astype(BF16)
    z = _dot(h, win_ref[...])
    a_val, a_gate, b_gate, c_gate, v = _split_z(z, d_a, d_b)
    ua_s[...] = a_val * jax.nn.sigmoid(a_gate)
    ub_s[...] = c_gate * v

    for t in range(t_new):
        acc = jnp.broadcast_to(cab_ref[...], (n_seq, d_a))
        for k in range(CONV_A_WIDTH):
            acc = acc + caw_ref[k:k + 1, :] * ext_a(t + k)
        ycat[rows(t), 0:d_a] = _layernorm_silu(acc, lng_ref[...], lnb_ref[...])
        acc = None
        for k in range(CONV_B_WIDTH):
            term = cbw_ref[k:k + 1, :] * ext_b(t + k)
            acc = term if acc is None else acc + term
        ycat[rows(t), d_a:d_a + d_b] = acc

    ya = ycat[:, :d_a]
    yb = b_gate * ycat[:, d_a:]
    out = _dot(ya.astype(BF16), wout_ref[:d_a, :]) + _dot(yb.astype(BF16), wout_ref[d_a:, :])
    o_ref[...] = x + out

    for j in range(hist_a):
        na_ref[j] = ext_a(j + t_new)
    for j in range(hist_b):
        nb_ref[j] = ext_b(j + t_new)


def _mix_ab_sample(x, hist_a, hist_b, g, w_in, caw, cab, lng, lnb, cbw, w_out):
    n, d = x.shape
    b = hist_a.shape[1]
    d_a = caw.shape[1]
    d_b = cbw.shape[1]
    row = lambda a: a.reshape(1, -1)
    ins = [x, hist_a, hist_b, row(g), w_in, caw, row(cab), row(lng), row(lnb), cbw, w_out]
    return pl.pallas_call(
        functools.partial(_mix_ab_sample_kernel, n_seq=b, t_new=n // b, d_a=d_a, d_b=d_b),
        out_shape=(jax.ShapeDtypeStruct((n, d), F32),
                   jax.ShapeDtypeStruct(hist_a.shape, F32),
                   jax.ShapeDtypeStruct(hist_b.shape, F32)),
        scratch_shapes=[pltpu.VMEM((n, d_a), F32), pltpu.VMEM((n, d_b), F32),
                        pltpu.VMEM((n, d_a + d_b), F32)],
        compiler_params=pltpu.CompilerParams(vmem_limit_bytes=VMEM_LIMIT_BYTES),
        name="mix_ab_sample",
    )(*ins)


P_PAD = 16


def _pool_mix(pooled, pw_ref, pb_ref, ps_ref, group):
    outs = []
    for gi in range(len(POOL_WINDOWS)):
        cs = slice(gi * group, (gi + 1) * group)
        outs.append(_dot(pooled[:, cs].astype(BF16), pw_ref[gi]))
    mixed = jnp.concatenate(outs, axis=-1)
    return (mixed + pb_ref[...]) * ps_ref[...]


def _pool_tile(x_ref, t, first_step, in_refs, state_refs, scratch_refs, *, tile, tiles_per_seq, d, group):
    del first_step
    g_ref, pw_ref, pb_ref, ps_ref = in_refs
    (np_ref,) = state_refs
    exth, pooled_s = scratch_refs

    @pl.when(t == 0)
    def _():
        exth[0:P_PAD, :] = jnp.zeros((P_PAD, d), F32)

    exth[P_PAD:P_PAD + tile, :] = _rms(x_ref[...], g_ref[...])

    pos = (t * tile + jax.lax.broadcasted_iota(jnp.int32, (tile, group), 0)).astype(F32)
    for gi, w in enumerate(POOL_WINDOWS):
        assert w & (w - 1) == 0 and w <= P_PAD
        cs = slice(gi * group, (gi + 1) * group)
        s = exth[:, cs]
        shift = 1
        while shift < w:
            s = s + pltpu.roll(s, shift, 0)
            shift *= 2
        count = jnp.minimum(pos + 1.0, float(w))
        pooled_s[:, cs] = s[P_PAD:, :] / count - exth[P_PAD:P_PAD + tile, cs]

    y = x_ref[...] + _pool_mix(pooled_s[...], pw_ref, pb_ref, ps_ref, group)

    exth[0:P_PAD, :] = exth[tile:tile + P_PAD, :]

    @pl.when(t == tiles_per_seq - 1)
    def _():
        np_ref[...] = exth[P_PAD - POOL_HIST:P_PAD, :]

    return y


def _pool_prompt(n_seq, seq_len, d, g, pw, pb, ps):
    group = pw.shape[-1]
    tile = min(FFN_TILE, seq_len)
    assert tile >= P_PAD
    row = lambda a: a.reshape(1, -1)
    return _Mixer(
        name="pool",
        tile_fn=functools.partial(_pool_tile, tile=tile, tiles_per_seq=seq_len // tile, d=d, group=group),
        ins=(row(g), pw, row(pb), row(ps)),
        state_shapes=((n_seq, POOL_HIST, d),),
        scratch=(pltpu.VMEM((P_PAD + tile, d), F32), pltpu.VMEM((tile, d), F32)),
        seq_len=seq_len)


def _pool_sample_kernel(x_ref, hp_ref, g_ref, pw_ref, pb_ref, ps_ref, o_ref, np_ref, h_s, pooled_s,
                        *, n_seq, t_new, pos0, group):
    def rows(t):
        return slice(t * n_seq, (t + 1) * n_seq)

    def ext(j, cs):
        return hp_ref[j, :, cs] if j < POOL_HIST else h_s[rows(j - POOL_HIST), cs]

    x = x_ref[...]
    h_s[...] = _rms(x, g_ref[...])

    for t in range(t_new):
        for gi, w in enumerate(POOL_WINDOWS):
            cs = slice(gi * group, (gi + 1) * group)
            s = ext(POOL_HIST + t, cs)
            for j in range(1, w):
                s = s + ext(POOL_HIST + t - j, cs)
            count = float(min(pos0 + t + 1, w))
            pooled_s[rows(t), cs] = s / count - ext(POOL_HIST + t, cs)

    o_ref[...] = x + _pool_mix(pooled_s[...], pw_ref, pb_ref, ps_ref, group)

    full = slice(None)
    for j in range(POOL_HIST):
        np_ref[j] = ext(j + t_new, full)


def _pool_sample(x, hist, pos0, g, pw, pb, ps):
    n, d = x.shape
    b = hist.shape[1]
    group = pw.shape[-1]
    row = lambda a: a.reshape(1, -1)
    return pl.pallas_call(
        functools.partial(_pool_sample_kernel, n_seq=b, t_new=n // b, pos0=pos0, group=group),
        out_shape=(jax.ShapeDtypeStruct((n, d), F32), jax.ShapeDtypeStruct(hist.shape, F32)),
        scratch_shapes=[pltpu.VMEM((n, d), F32), pltpu.VMEM((n, d), F32)],
        compiler_params=pltpu.CompilerParams(vmem_limit_bytes=VMEM_LIMIT_BYTES),
        name="pool_sample",
    )(x, hist, row(g), pw, row(pb), row(ps))


def kernel(x_prompt, x_sample, state_conv_a, state_conv_b, state_pool, norm_g, ffn_w_in, ffn_w_out,
           mix_w_in, conv_a_w, conv_a_b, ln_a_g, ln_a_b, conv_b_w, mix_w_out, pool_w, pool_b,
           pool_scale, final_norm_g):
    depth = norm_g.shape[0]
    d = x_prompt.shape[-1]
    bs, ts, _ = x_sample.shape
    step_major = lambda a: jnp.swapaxes(a, 0, 1)

    n_even, n_odd = mix_w_in.shape[0], pool_w.shape[0]
    n_groups, group = pool_w.shape[1], pool_w.shape[2]
    pool_w2 = pool_w.reshape(n_odd, n_groups * group, group)
    mixer_casts = ([(mix_w_in, (e,)) for e in range(n_even)] + [(mix_w_out, (e,)) for e in range(n_even)]
                   + [(pool_w2, (o,)) for o in range(n_odd)])
    ffn_w = {(0, 0): (ffn_w_in[0, 0].astype(BF16), ffn_w_out[0, 0].astype(BF16))}
    mix_w_in_h, mix_w_out_h, pool_w_h = [], [], []

    def ffn(xp, xs, l, which, final_g=None, mixer=None):
        nxt = (l, 1) if which == 0 else (l + 1, 0)
        cast = [(ffn_w_in, nxt), (ffn_w_out, nxt)] if nxt[0] < depth else []
        if (l, which) == (0, 0):
            cast = cast + mixer_casts
        yp, ys, states, casts = _ffn(xp, norm_g[l, 2 * which], *ffn_w[(l, which)], final_g, cast, mixer,
                                     extra=xs, tile=FFN_TILE if mixer else PLAIN_FFN_TILE)
        if nxt[0] < depth:
            ffn_w[nxt] = casts[:2]
        if (l, which) == (0, 0):
            rest = casts[len(casts) - len(mixer_casts):]
            mix_w_in_h.extend(rest[:n_even])
            mix_w_out_h.extend(rest[n_even:2 * n_even])
            pool_w_h.extend(w.reshape(n_groups, group, group) for w in rest[2 * n_even:])
        return yp, ys, states

    bp, tp, _ = x_prompt.shape
    xp, xs = x_prompt.reshape(bp * tp, d), step_major(x_sample).reshape(ts * bs, d)
    pa, pb, pp, sa, sb, sp = [], [], [], [], [], []
    for l in range(depth):
        final_g = final_norm_g if l == depth - 1 else None
        xp, xs, _ = ffn(xp, xs, l, 0)
        if l % 2 == 0:
            e = l // 2
            wts = (norm_g[l, 1], mix_w_in_h[e], conv_a_w[e], conv_a_b[e], ln_a_g[e], ln_a_b[e],
                   conv_b_w[e], mix_w_out_h[e])
            xs, na, nb = _mix_ab_sample(xs, step_major(state_conv_a[e]), step_major(state_conv_b[e]),
                                        *wts)
            sa.append(step_major(na))
            sb.append(step_major(nb))
            xp, xs, (na, nb) = ffn(xp, xs, l, 1, final_g, _mix_ab_prompt(bp, tp, *wts))
            pa.append(na)
            pb.append(nb)
        else:
            o = l // 2
            wts = (norm_g[l, 1], pool_w_h[o], pool_b[o], pool_scale[o])
            xs, npool = _pool_sample(xs, step_major(state_pool[o]), PAST_LEN, *wts)
            sp.append(step_major(npool))
            xp, xs, (npool,) = ffn(xp, xs, l, 1, final_g, _pool_prompt(bp, tp, d, *wts))
            pp.append(npool)

    xp = xp.reshape(bp, tp, d)
    xs = step_major(xs.reshape(ts, bs, d))
    st = lambda parts: jnp.stack(parts, axis=0)
    return (xp, xs, st(pa), st(pb), st(pp), st(sa), st(sb), st(sp))
```

```python
import functools
from typing import Callable, NamedTuple

import jax
import jax.numpy as jnp
from jax.experimental import pallas as pl
from jax.experimental.pallas import tpu as pltpu

F32 = jnp.float32
BF16 = jnp.bfloat16

EPS = 1e-6
CONV_A_WIDTH = 31
CONV_B_WIDTH = 3
POOL_WINDOWS = (2, 4, 8, 16)
POOL_HIST = max(POOL_WINDOWS) - 1
PAST_LEN = 16384

SUBLANES = 8
BF16_SUBLANES = 16
LANES = 128
VMEM_LIMIT_BYTES = 56 * 1024 * 1024

FFN_TILE = 512
PLAIN_FFN_TILE = 512
FFN_CHUNK = 256
CONV_ROWS = 32


def _rms(x, g):
    ms = jnp.mean(x * x, axis=-1, keepdims=True)
    return x * jax.lax.rsqrt(ms + EPS) * g


def _dot(a, b):
    return jnp.dot(a, b, preferred_element_type=F32)


def _const_spec(shape):
    nd = len(shape)
    return pl.BlockSpec(shape, lambda *_: (0,) * nd, pipeline_mode=pl.Buffered(1))


def _params(sem):
    return pltpu.CompilerParams(dimension_semantics=sem, vmem_limit_bytes=VMEM_LIMIT_BYTES)


def _cast_side(cast_src, cast_dst):
    for src, dst in zip(cast_src, cast_dst):
        dst[...] = src[...].astype(BF16)


def _ffn_tile(x_ref, g_ref, win_ref, wout_ref, fg_ref, *, d_ff, chunk):
    h = _rms(x_ref[...], g_ref[...]).astype(BF16)
    acc = None
    for c in range(d_ff // chunk):
        lo = c * chunk
        gate = _dot(h, win_ref[:, lo:lo + chunk])
        up = _dot(h, win_ref[:, d_ff + lo:d_ff + lo + chunk])
        a = (gate * jax.nn.sigmoid(gate) * up).astype(BF16)
        p = _dot(a, wout_ref[lo:lo + chunk, :])
        acc = p if acc is None else acc + p
    y = x_ref[...] + 0.5 * acc
    if fg_ref is not None:
        y = _rms(y, fg_ref[...])
    return y


class _Mixer(NamedTuple):
    name: str
    tile_fn: Callable
    ins: tuple
    state_shapes: tuple
    scratch: tuple
    seq_len: int


def _ffn_kernel(*refs, d_ff, chunk, final, n_cast, mixer, tiles_per_seq, n_tiles, has_extra):
    refs = list(refs)
    take = lambda k: [refs.pop(0) for _ in range(k)]
    (x_ref,) = take(1)
    extra_ref = take(1)[0] if has_extra else None
    mix_in = take(len(mixer.ins)) if mixer else []
    g_ref, win_ref, wout_ref = take(3)
    fg_ref = take(1)[0] if final else None
    cast_src = take(n_cast)
    (o_ref,) = take(1)
    extra_o_ref = take(1)[0] if has_extra else None
    state = take(len(mixer.state_shapes)) if mixer else []
    cast_dst = take(n_cast)
    mix_scratch = take(len(mixer.scratch)) if mixer else []
    ffn = functools.partial(_ffn_tile, g_ref=g_ref, win_ref=win_ref, wout_ref=wout_ref, fg_ref=fg_ref,
                            d_ff=d_ff, chunk=chunk)
    i = pl.program_id(0)
    _cast_side(cast_src, cast_dst)

    def main_step():
        if mixer:
            (xm,) = refs
            xm[...] = mixer.tile_fn(x_ref, jax.lax.rem(i, tiles_per_seq), i == 0, mix_in, state, mix_scratch)
            o_ref[...] = ffn(xm)
        else:
            o_ref[...] = ffn(x_ref)

    if not has_extra:
        main_step()
        return

    pl.when(i < n_tiles)(main_step)

    @pl.when(i == n_tiles)
    def _():
        extra_o_ref[...] = ffn(extra_ref)


def _cast_specs(w, sel, n_steps):
    r, c = w.shape[len(sel):]
    n_blocks = n_steps
    while r % n_blocks or (r // n_blocks) % BF16_SUBLANES:
        assert n_blocks % 2 == 0, (r, n_steps)
        n_blocks //= 2
    rep = n_steps // n_blocks
    rows = r // n_blocks
    blk = lambda i: jnp.minimum(i, n_steps - 1) // rep
    in_spec = pl.BlockSpec((None,) * len(sel) + (rows, c), lambda i: tuple(sel) + (blk(i), 0))
    out_spec = pl.BlockSpec((rows, c), lambda i: (blk(i), 0))
    return in_spec, out_spec, jax.ShapeDtypeStruct((r, c), BF16)


def _ffn(x, g, w_in, w_out, final_g=None, cast=(), mixer=None, extra=None, tile=FFN_TILE):
    n, d = x.shape
    d_ff = w_out.shape[0]
    tile = min(tile, n)
    assert n % tile == 0 and d_ff % FFN_CHUNK == 0
    n_tiles = n // tile
    final = final_g is not None
    has_extra = extra is not None
    tiles_per_seq = 1
    if mixer:
        assert mixer.seq_len % tile == 0
        tiles_per_seq = mixer.seq_len // tile
    main_tile = lambda i: jnp.minimum(i, n_tiles - 1)
    x_spec = o_spec = pl.BlockSpec((tile, d), lambda i: (main_tile(i), 0))
    ins, specs = [x], [x_spec]
    out_shapes, out_specs = [jax.ShapeDtypeStruct((n, d), F32)], [o_spec]
    if has_extra:
        ins.append(extra)
        specs.append(pl.BlockSpec(extra.shape, lambda i: (0, 0)))
        out_shapes.append(jax.ShapeDtypeStruct(extra.shape, F32))
        out_specs.append(pl.BlockSpec(extra.shape, lambda i: (0, 0)))
    if mixer:
        ins += list(mixer.ins)
        specs += [_const_spec(a.shape) for a in mixer.ins]
    ins += [g.reshape(1, d), w_in, w_out]
    specs += [_const_spec((1, d)), _const_spec(w_in.shape), _const_spec(w_out.shape)]
    if final:
        ins.append(final_g.reshape(1, d))
        specs.append(_const_spec((1, d)))
    scratch = []
    if mixer:
        for shp in mixer.state_shapes:
            out_shapes.append(jax.ShapeDtypeStruct(shp, F32))
            out_specs.append(pl.BlockSpec((None,) + tuple(shp[1:]),
                                          lambda i: (main_tile(i) // tiles_per_seq, 0, 0)))
        scratch = list(mixer.scratch) + [pltpu.VMEM((tile, d), F32)]
    cast_ins, cast_specs_, cast_out_specs, cast_out_shapes = [], [], [], []
    for w, sel in cast:
        in_spec, out_spec, out_shape = _cast_specs(w, sel, n_tiles)
        cast_ins.append(w)
        cast_specs_.append(in_spec)
        cast_out_specs.append(out_spec)
        cast_out_shapes.append(out_shape)
    n_state = len(mixer.state_shapes) if mixer else 0
    outs = list(pl.pallas_call(
        functools.partial(_ffn_kernel, d_ff=d_ff, chunk=FFN_CHUNK, final=final, n_cast=len(cast),
                          mixer=mixer, tiles_per_seq=tiles_per_seq, n_tiles=n_tiles, has_extra=has_extra),
        out_shape=tuple(out_shapes + cast_out_shapes),
        grid=(n_tiles + (1 if has_extra else 0),),
        in_specs=specs + cast_specs_,
        out_specs=tuple(out_specs + cast_out_specs),
        scratch_shapes=scratch,
        compiler_params=_params(("arbitrary",)),
        name=(mixer.name + "_" if mixer else "") + ("ffn_final" if final else "ffn"),
    )(*ins, *cast_ins))
    y = outs.pop(0)
    y_extra = outs.pop(0) if has_extra else None
    return y, y_extra, tuple(outs[:n_state]), tuple(outs[n_state:])


def _layernorm_silu(y, g, b):
    mu = jnp.mean(y, axis=-1, keepdims=True)
    yc = y - mu
    yn = yc * jax.lax.rsqrt(jnp.mean(yc * yc, axis=-1, keepdims=True) + EPS) * g + b
    return yn * jax.nn.sigmoid(yn)


def _split_z(z, d_a, d_b):
    a_val = z[:, :d_a]
    a_gate = z[:, d_a:2 * d_a]
    b_gate = z[:, 2 * d_a:2 * d_a + d_b]
    c_gate = z[:, 2 * d_a + d_b:2 * d_a + 2 * d_b]
    v = z[:, 2 * d_a + 2 * d_b:]
    return a_val, a_gate, b_gate, c_gate, v


A_PAD = 32
B_PAD = 8
SKEW_TAPS = CONV_A_WIDTH + SUBLANES - 1


def _mix_ab_tile(x_ref, t, first_step, in_refs, state_refs, scratch_refs, *, tile, tiles_per_seq, d_a, d_b):
    g_ref, win_ref, caw_ref, cab_ref, lng_ref, lnb_ref, cbw_ref, wout_ref = in_refs
    na_ref, nb_ref = state_refs
    exta, extb, ycat, wskew, bgate = scratch_refs

    @pl.when(first_step)
    def _():
        wskew[...] = jnp.zeros(wskew.shape, F32)
        for dd in range(SKEW_TAPS):
            for p in range(SUBLANES):
                if 0 <= dd - p < CONV_A_WIDTH:
                    wskew[dd, p:p + 1, :] = caw_ref[dd - p:dd - p + 1, :]

    @pl.when(t == 0)
    def _():
        exta[0:A_PAD, :] = jnp.zeros((A_PAD, d_a), F32)
        extb[0:B_PAD, :] = jnp.zeros((B_PAD, d_b), F32)

    h = _rms(x_ref[...], g_ref[...]).astype(BF16)
    z = _dot(h, win_ref[...])
    a_val, a_gate, b_gate, c_gate, v = _split_z(z, d_a, d_b)
    exta[A_PAD:A_PAD + tile, :] = a_val * jax.nn.sigmoid(a_gate)
    extb[B_PAD:B_PAD + tile, :] = c_gate * v
    bgate[...] = b_gate

    off_a = A_PAD - (CONV_A_WIDTH - 1)
    off_b = B_PAD - (CONV_B_WIDTH - 1)
    for r0 in range(0, tile, CONV_ROWS):
        rs = slice(r0, r0 + CONV_ROWS)
        for c0 in range(0, d_a, LANES):
            cs = slice(c0, c0 + LANES)
            n_acc = CONV_ROWS // SUBLANES
            accs = [jnp.broadcast_to(cab_ref[:, cs], (SUBLANES, LANES))] * n_acc
            for dd in range(SKEW_TAPS):
                wv = wskew[dd, :, cs]
                for m in range(n_acc):
                    src = r0 + SUBLANES * m + off_a + dd
                    accs[m] = accs[m] + wv * jnp.broadcast_to(exta[src:src + 1, cs], (SUBLANES, LANES))
            for m in range(n_acc):
                ycat[r0 + SUBLANES * m:r0 + SUBLANES * (m + 1), cs] = accs[m]
        ycat[rs, :d_a] = _layernorm_silu(ycat[rs, :d_a], lng_ref[...], lnb_ref[...])
        for c0 in range(0, d_b, LANES):
            cs = slice(c0, c0 + LANES)
            acc = None
            for k in range(CONV_B_WIDTH):
                term = cbw_ref[k:k + 1, cs] * extb[r0 + off_b + k:r0 + off_b + k + CONV_ROWS, cs]
                acc = term if acc is None else acc + term
            ycat[rs, d_a + c0:d_a + c0 + LANES] = bgate[rs, cs] * acc

    out = _dot(ycat[...].astype(BF16), wout_ref[...])

    exta[0:A_PAD, :] = exta[tile:tile + A_PAD, :]
    extb[0:B_PAD, :] = extb[tile:tile + B_PAD, :]

    @pl.when(t == tiles_per_seq - 1)
    def _():
        na_ref[...] = exta[off_a:A_PAD, :]
        nb_ref[...] = extb[off_b:B_PAD, :]

    return x_ref[...] + out


def _mix_ab_prompt(n_seq, seq_len, g, w_in, caw, cab, lng, lnb, cbw, w_out):
    d_a = caw.shape[1]
    d_b = cbw.shape[1]
    tile = min(FFN_TILE, seq_len)
    assert tile >= A_PAD and tile % CONV_ROWS == 0
    row = lambda a: a.reshape(1, -1)
    return _Mixer(
        name="mix_ab",
        tile_fn=functools.partial(_mix_ab_tile, tile=tile, tiles_per_seq=seq_len // tile, d_a=d_a, d_b=d_b),
        ins=(row(g), w_in, caw, row(cab), row(lng), row(lnb), cbw, w_out),
        state_shapes=((n_seq, CONV_A_WIDTH - 1, d_a), (n_seq, CONV_B_WIDTH - 1, d_b)),
        scratch=(pltpu.VMEM((A_PAD + tile, d_a), F32), pltpu.VMEM((B_PAD + tile, d_b), F32),
                 pltpu.VMEM((tile, d_a + d_b), F32), pltpu.VMEM((SKEW_TAPS, SUBLANES, d_a), F32),
                 pltpu.VMEM((tile, d_b), F32)),
        seq_len=seq_len)


def _mix_ab_sample_kernel(x_ref, ha_ref, hb_ref, g_ref, win_ref, caw_ref, cab_ref, lng_ref, lnb_ref,
                          cbw_ref, wout_ref, o_ref, na_ref, nb_ref, ua_s, ub_s, ycat,
                          *, n_seq, t_new, d_a, d_b):
    hist_a = CONV_A_WIDTH - 1
    hist_b = CONV_B_WIDTH - 1

    def rows(t):
        return slice(t * n_seq, (t + 1) * n_seq)

    def ext_a(j):
        return ha_ref[j] if j < hist_a else ua_s[rows(j - hist_a), :]

    def ext_b(j):
        return hb_ref[j] if j < hist_b else ub_s[rows(j - hist_b), :]

    x = x_ref[...]
    h = _rms(x, g_ref[...]).astype(BF16)
    z = _dot(h, win_ref[...])
    a_val, a_gate, b_gate, c_gate, v = _split_z(z, d_a, d_b)
    ua_s[...] = a_val * jax.nn.sigmoid(a_gate)
    ub_s[...] = c_gate * v

    for t in range(t_new):
        acc = jnp.broadcast_to(cab_ref[...], (n_seq, d_a))
        for k in range(CONV_A_WIDTH):
            acc = acc + caw_ref[k:k + 1, :] * ext_a(t + k)
        ycat[rows(t), 0:d_a] = _layernorm_silu(acc, lng_ref[...], lnb_ref[...])
        acc = None
        for k in range(CONV_B_WIDTH):
            term = cbw_ref[k:k + 1, :] * ext_b(t + k)
            acc = term if acc is None else acc + term
        ycat[rows(t), d_a:d_a + d_b] = acc

    ya = ycat[:, :d_a]
    yb = b_gate * ycat[:, d_a:]
    out = _dot(ya.astype(BF16), wout_ref[:d_a, :]) + _dot(yb.astype(BF16), wout_ref[d_a:, :])
    o_ref[...] = x + out

    for j in range(hist_a):
        na_ref[j] = ext_a(j + t_new)
    for j in range(hist_b):
        nb_ref[j] = ext_b(j + t_new)


def _mix_ab_sample(x, hist_a, hist_b, g, w_in, caw, cab, lng, lnb, cbw, w_out):
    n, d = x.shape
    b = hist_a.shape[1]
    d_a = caw.shape[1]
    d_b = cbw.shape[1]
    row = lambda a: a.reshape(1, -1)
    ins = [x, hist_a, hist_b, row(g), w_in, caw, row(cab), row(lng), row(lnb), cbw, w_out]
    return pl.pallas_call(
        functools.partial(_mix_ab_sample_kernel, n_seq=b, t_new=n // b, d_a=d_a, d_b=d_b),
        out_shape=(jax.ShapeDtypeStruct((n, d), F32),
                   jax.ShapeDtypeStruct(hist_a.shape, F32),
                   jax.ShapeDtypeStruct(hist_b.shape, F32)),
        scratch_shapes=[pltpu.VMEM((n, d_a), F32), pltpu.VMEM((n, d_b), F32),
                        pltpu.VMEM((n, d_a + d_b), F32)],
        compiler_params=pltpu.CompilerParams(vmem_limit_bytes=VMEM_LIMIT_BYTES),
        name="mix_ab_sample",
    )(*ins)


P_PAD = 16


def _pool_mix(pooled, pw_ref, pb_ref, ps_ref, group):
    outs = []
    for gi in range(len(POOL_WINDOWS)):
        cs = slice(gi * group, (gi + 1) * group)
        outs.append(_dot(pooled[:, cs].astype(BF16), pw_ref[gi]))
    mixed = jnp.concatenate(outs, axis=-1)
    return (mixed + pb_ref[...]) * ps_ref[...]


def _pool_tile(x_ref, t, first_step, in_refs, state_refs, scratch_refs, *, tile, tiles_per_seq, d, group):
    del first_step
    g_ref, pw_ref, pb_ref, ps_ref = in_refs
    (np_ref,) = state_refs
    exth, pooled_s = scratch_refs

    @pl.when(t == 0)
    def _():
        exth[0:P_PAD, :] = jnp.zeros((P_PAD, d), F32)

    exth[P_PAD:P_PAD + tile, :] = _rms(x_ref[...], g_ref[...])

    pos = (t * tile + jax.lax.broadcasted_iota(jnp.int32, (tile, group), 0)).astype(F32)
    for gi, w in enumerate(POOL_WINDOWS):
        assert w & (w - 1) == 0 and w <= P_PAD
        cs = slice(gi * group, (gi + 1) * group)
        s = exth[:, cs]
        shift = 1
        while shift < w:
            s = s + pltpu.roll(s, shift, 0)
            shift *= 2
        count = jnp.minimum(pos + 1.0, float(w))
        pooled_s[:, cs] = s[P_PAD:, :] / count - exth[P_PAD:P_PAD + tile, cs]

    y = x_ref[...] + _pool_mix(pooled_s[...], pw_ref, pb_ref, ps_ref, group)

    exth[0:P_PAD, :] = exth[tile:tile + P_PAD, :]

    @pl.when(t == tiles_per_seq - 1)
    def _():
        np_ref[...] = exth[P_PAD - POOL_HIST:P_PAD, :]

    return y


def _pool_prompt(n_seq, seq_len, d, g, pw, pb, ps):
    group = pw.shape[-1]
    tile = min(FFN_TILE, seq_len)
    assert tile >= P_PAD
    row = lambda a: a.reshape(1, -1)
    return _Mixer(
        name="pool",
        tile_fn=functools.partial(_pool_tile, tile=tile, tiles_per_seq=seq_len // tile, d=d, group=group),
        ins=(row(g), pw, row(pb), row(ps)),
        state_shapes=((n_seq, POOL_HIST, d),),
        scratch=(pltpu.VMEM((P_PAD + tile, d), F32), pltpu.VMEM((tile, d), F32)),
        seq_len=seq_len)


def _pool_sample_kernel(x_ref, hp_ref, g_ref, pw_ref, pb_ref, ps_ref, o_ref, np_ref, h_s, pooled_s,
                        *, n_seq, t_new, pos0, group):
    def rows(t):
        return slice(t * n_seq, (t + 1) * n_seq)

    def ext(j, cs):
        return hp_ref[j, :, cs] if j < POOL_HIST else h_s[rows(j - POOL_HIST), cs]

    x = x_ref[...]
    h_s[...] = _rms(x, g_ref[...])

    for t in range(t_new):
        for gi, w in enumerate(POOL_WINDOWS):
            cs = slice(gi * group, (gi + 1) * group)
            s = ext(POOL_HIST + t, cs)
            for j in range(1, w):
                s = s + ext(POOL_HIST + t - j, cs)
            count = float(min(pos0 + t + 1, w))
            pooled_s[rows(t), cs] = s / count - ext(POOL_HIST + t, cs)

    o_ref[...] = x + _pool_mix(pooled_s[...], pw_ref, pb_ref, ps_ref, group)

    full = slice(None)
    for j in range(POOL_HIST):
        np_ref[j] = ext(j + t_new, full)


def _pool_sample(x, hist, pos0, g, pw, pb, ps):
    n, d = x.shape
    b = hist.shape[1]
    group = pw.shape[-1]
    row = lambda a: a.reshape(1, -1)
    return pl.pallas_call(
        functools.partial(_pool_sample_kernel, n_seq=b, t_new=n // b, pos0=pos0, group=group),
        out_shape=(jax.ShapeDtypeStruct((n, d), F32), jax.ShapeDtypeStruct(hist.shape, F32)),
        scratch_shapes=[pltpu.VMEM((n, d), F32), pltpu.VMEM((n, d), F32)],
        compiler_params=pltpu.CompilerParams(vmem_limit_bytes=VMEM_LIMIT_BYTES),
        name="pool_sample",
    )(x, hist, row(g), pw, row(pb), row(ps))


def kernel(x_prompt, x_sample, state_conv_a, state_conv_b, state_pool, norm_g, ffn_w_in, ffn_w_out,
           mix_w_in, conv_a_w, conv_a_b, ln_a_g, ln_a_b, conv_b_w, mix_w_out, pool_w, pool_b,
           pool_scale, final_norm_g):
    depth = norm_g.shape[0]
    d = x_prompt.shape[-1]
    bs, ts, _ = x_sample.shape
    step_major = lambda a: jnp.swapaxes(a, 0, 1)

    n_even, n_odd = mix_w_in.shape[0], pool_w.shape[0]
    n_groups, group = pool_w.shape[1], pool_w.shape[2]
    pool_w2 = pool_w.reshape(n_odd, n_groups * group, group)
    mixer_casts = ([(mix_w_in, (e,)) for e in range(n_even)] + [(mix_w_out, (e,)) for e in range(n_even)]
                   + [(pool_w2, (o,)) for o in range(n_odd)])
    ffn_w = {(0, 0): (ffn_w_in[0, 0].astype(BF16), ffn_w_out[0, 0].astype(BF16))}
    mix_w_in_h, mix_w_out_h, pool_w_h = [], [], []

    def ffn(xp, xs, l, which, final_g=None, mixer=None):
        nxt = (l, 1) if which == 0 else (l + 1, 0)
        cast = [(ffn_w_in, nxt), (ffn_w_out, nxt)] if nxt[0] < depth else []
        if (l, which) == (0, 0):
            cast = cast + mixer_casts
        yp, ys, states, casts = _ffn(xp, norm_g[l, 2 * which], *ffn_w[(l, which)], final_g, cast, mixer,
                                     extra=xs, tile=FFN_TILE if mixer else PLAIN_FFN_TILE)
        if nxt[0] < depth:
            ffn_w[nxt] = casts[:2]
        if (l, which) == (0, 0):
            rest = casts[len(casts) - len(mixer_casts):]
            mix_w_in_h.extend(rest[:n_even])
            mix_w_out_h.extend(rest[n_even:2 * n_even])
            pool_w_h.extend(w.reshape(n_groups, group, group) for w in rest[2 * n_even:])
        return yp, ys, states

    bp, tp, _ = x_prompt.shape
    xp, xs = x_prompt.reshape(bp * tp, d), step_major(x_sample).reshape(ts * bs, d)
    pa, pb, pp, sa, sb, sp = [], [], [], [], [], []
    for l in range(depth):
        final_g = final_norm_g if l == depth - 1 else None
        xp, xs, _ = ffn(xp, xs, l, 0)
        if l % 2 == 0:
            e = l // 2
            wts = (norm_g[l, 1], mix_w_in_h[e], conv_a_w[e], conv_a_b[e], ln_a_g[e], ln_a_b[e],
                   conv_b_w[e], mix_w_out_h[e])
            xs, na, nb = _mix_ab_sample(xs, step_major(state_conv_a[e]), step_major(state_conv_b[e]),
                                        *wts)
            sa.append(step_major(na))
            sb.append(step_major(nb))
            xp, xs, (na, nb) = ffn(xp, xs, l, 1, final_g, _mix_ab_prompt(bp, tp, *wts))
            pa.append(na)
            pb.append(nb)
        else:
            o = l // 2
            wts = (norm_g[l, 1], pool_w_h[o], pool_b[o], pool_scale[o])
            xs, npool = _pool_sample(xs, step_major(state_pool[o]), PAST_LEN, *wts)
            sp.append(step_major(npool))
            xp, xs, (npool,) = ffn(xp, xs, l, 1, final_g, _pool_prompt(bp, tp, d, *wts))
            pp.append(npool)

    xp = xp.reshape(bp, tp, d)
    xs = step_major(xs.reshape(ts, bs, d))
    st = lambda parts: jnp.stack(parts, axis=0)
    return (xp, xs, st(pa), st(pb), st(pp), st(sa), st(sb), st(sp))
```

```python
import functools
from typing import Callable, NamedTuple

import jax
import jax.numpy as jnp
from jax.experimental import pallas as pl
from jax.experimental.pallas import tpu as pltpu

F32 = jnp.float32
BF16 = jnp.bfloat16

EPS = 1e-6
CONV_A_WIDTH = 31
CONV_B_WIDTH = 3
POOL_WINDOWS = (2, 4, 8, 16)
POOL_HIST = max(POOL_WINDOWS) - 1
PAST_LEN = 16384

SUBLANES = 8
BF16_SUBLANES = 16
LANES = 128
VMEM_LIMIT_BYTES = 56 * 1024 * 1024

FFN_TILE = 512
PLAIN_FFN_TILE = 512
WEIGHT_LOAD_STEPS = 8
FFN_CHUNK = 256
CONV_ROWS = 32


def _rms(x, g):
    ms = jnp.mean(x * x, axis=-1, keepdims=True)
    return x * jax.lax.rsqrt(ms + EPS) * g


def _dot(a, b):
    return jnp.dot(a, b, preferred_element_type=F32)


def _const_spec(shape):
    nd = len(shape)
    return pl.BlockSpec(shape, lambda *_: (0,) * nd, pipeline_mode=pl.Buffered(1))


def _params(sem):
    return pltpu.CompilerParams(dimension_semantics=sem, vmem_limit_bytes=VMEM_LIMIT_BYTES)


def _cast_side(cast_src, cast_dst):
    for src, dst in zip(cast_src, cast_dst):
        dst[...] = src[...].astype(BF16)


def _ffn_tile(x_ref, g_ref, win_ref, wout_ref, fg_ref, *, d_ff, chunk):
    h = _rms(x_ref[...], g_ref[...]).astype(BF16)
    acc = None
    for c in range(d_ff // chunk):
        lo = c * chunk
        gate = _dot(h, win_ref[:, lo:lo + chunk])
        up = _dot(h, win_ref[:, d_ff + lo:d_ff + lo + chunk])
        a = (gate * jax.nn.sigmoid(gate) * up).astype(BF16)
        p = _dot(a, wout_ref[lo:lo + chunk, :])
        acc = p if acc is None else acc + p
    y = x_ref[...] + 0.5 * acc
    if fg_ref is not None:
        y = _rms(y, fg_ref[...])
    return y


class _Mixer(NamedTuple):
    name: str
    tile_fn: Callable
    ins: tuple
    state_shapes: tuple
    scratch: tuple
    seq_len: int


def _ffn_kernel(*refs, d_ff, chunk, final, n_cast, mixer, tiles_per_seq, n_tiles, has_extra, n_load):
    refs = list(refs)
    take = lambda k: [refs.pop(0) for _ in range(k)]
    (x_ref,) = take(1)
    extra_ref = take(1)[0] if has_extra else None
    mix_in = take(len(mixer.ins)) if mixer else []
    g_ref, win_ref, wout_ref = take(3)
    fg_ref = take(1)[0] if final else None
    cast_src = take(n_cast)
    (o_ref,) = take(1)
    extra_o_ref = take(1)[0] if has_extra else None
    state = take(len(mixer.state_shapes)) if mixer else []
    cast_dst = take(n_cast)
    i = pl.program_id(0)
    step = i - n_load
    if n_load:
        win_s, wout_s = take(2)

        @pl.when(i < n_load)
        def _():
            for src, dst in ((win_ref, win_s), (wout_ref, wout_s)):
                rows = src.shape[0]
                dst[pl.ds(pl.multiple_of(i * rows, rows), rows), :] = src[...].astype(BF16)

        win_ref, wout_ref = win_s, wout_s
    mix_scratch = take(len(mixer.scratch)) if mixer else []
    ffn = functools.partial(_ffn_tile, g_ref=g_ref, win_ref=win_ref, wout_ref=wout_ref, fg_ref=fg_ref,
                            d_ff=d_ff, chunk=chunk)
    _cast_side(cast_src, cast_dst)

    def main_step():
        if mixer:
            (xm,) = refs
            xm[...] = mixer.tile_fn(x_ref, jax.lax.rem(step, tiles_per_seq), step == 0, mix_in, state,
                                    mix_scratch)
            o_ref[...] = ffn(xm)
        else:
            o_ref[...] = ffn(x_ref)

    if not (has_extra or n_load):
        main_step()
        return

    pl.when((step >= 0) & (step < n_tiles))(main_step)
    if has_extra:
        @pl.when(step == n_tiles)
        def _():
            extra_o_ref[...] = ffn(extra_ref)


def _cast_specs(w, sel, n_steps, first=0):
    r, c = w.shape[len(sel):]
    n_blocks = n_steps
    while r % n_blocks or (r // n_blocks) % BF16_SUBLANES:
        assert n_blocks % 2 == 0, (r, n_steps)
        n_blocks //= 2
    rep = n_steps // n_blocks
    rows = r // n_blocks
    blk = lambda i: jnp.clip(i - first, 0, n_steps - 1) // rep
    in_spec = pl.BlockSpec((None,) * len(sel) + (rows, c), lambda i: tuple(sel) + (blk(i), 0))
    out_spec = pl.BlockSpec((rows, c), lambda i: (blk(i), 0))
    return in_spec, out_spec, jax.ShapeDtypeStruct((r, c), BF16)


def _ffn(x, g, w_in, w_out, final_g=None, cast=(), mixer=None, extra=None, tile=FFN_TILE):
    n, d = x.shape
    own_f32 = isinstance(w_in, tuple)
    n_load = WEIGHT_LOAD_STEPS if own_f32 else 0
    w_shapes = [w[0].shape[len(w[1]):] if own_f32 else w.shape for w in (w_in, w_out)]
    d_ff = w_shapes[1][0]
    tile = min(tile, n)
    assert n % tile == 0 and d_ff % FFN_CHUNK == 0
    n_tiles = n // tile
    final = final_g is not None
    has_extra = extra is not None
    tiles_per_seq = 1
    if mixer:
        assert mixer.seq_len % tile == 0
        tiles_per_seq = mixer.seq_len // tile
    main_tile = lambda i: jnp.clip(i - n_load, 0, n_tiles - 1)
    x_spec = o_spec = pl.BlockSpec((tile, d), lambda i: (main_tile(i), 0))
    ins, specs = [x], [x_spec]
    out_shapes, out_specs = [jax.ShapeDtypeStruct((n, d), F32)], [o_spec]
    if has_extra:
        ins.append(extra)
        specs.append(pl.BlockSpec(extra.shape, lambda i: (0, 0)))
        out_shapes.append(jax.ShapeDtypeStruct(extra.shape, F32))
        out_specs.append(pl.BlockSpec(extra.shape, lambda i: (0, 0)))
    if mixer:
        ins += list(mixer.ins)
        specs += [_const_spec(a.shape) for a in mixer.ins]
    ins.append(g.reshape(1, d))
    specs.append(_const_spec((1, d)))
    scratch = []
    for w, (r, c) in zip((w_in, w_out), w_shapes):
        if own_f32:
            arr, sel = w
            assert r % n_load == 0 and (r // n_load) % BF16_SUBLANES == 0
            ins.append(arr)
            specs.append(pl.BlockSpec((None,) * len(sel) + (r // n_load, c),
                                      lambda i, sel=sel: tuple(sel) + (jnp.minimum(i, n_load - 1), 0)))
            scratch.append(pltpu.VMEM((r, c), BF16))
        else:
            ins.append(w)
            specs.append(_const_spec((r, c)))
    if final:
        ins.append(final_g.reshape(1, d))
        specs.append(_const_spec((1, d)))
    if mixer:
        for shp in mixer.state_shapes:
            out_shapes.append(jax.ShapeDtypeStruct(shp, F32))
            out_specs.append(pl.BlockSpec((None,) + tuple(shp[1:]),
                                          lambda i: (main_tile(i) // tiles_per_seq, 0, 0)))
        scratch += list(mixer.scratch) + [pltpu.VMEM((tile, d), F32)]
    cast_ins, cast_specs_, cast_out_specs, cast_out_shapes = [], [], [], []
    for w, sel in cast:
        in_spec, out_spec, out_shape = _cast_specs(w, sel, n_tiles, first=n_load)
        cast_ins.append(w)
        cast_specs_.append(in_spec)
        cast_out_specs.append(out_spec)
        cast_out_shapes.append(out_shape)
    n_state = len(mixer.state_shapes) if mixer else 0
    outs = list(pl.pallas_call(
        functools.partial(_ffn_kernel, d_ff=d_ff, chunk=FFN_CHUNK, final=final, n_cast=len(cast),
                          mixer=mixer, tiles_per_seq=tiles_per_seq, n_tiles=n_tiles, has_extra=has_extra,
                          n_load=n_load),
        out_shape=tuple(out_shapes + cast_out_shapes),
        grid=(n_load + n_tiles + (1 if has_extra else 0),),
        in_specs=specs + cast_specs_,
        out_specs=tuple(out_specs + cast_out_specs),
        scratch_shapes=scratch,
        compiler_params=_params(("arbitrary",)),
        name=(mixer.name + "_" if mixer else "") + ("ffn_final" if final else "ffn"),
    )(*ins, *cast_ins))
    y = outs.pop(0)
    y_extra = outs.pop(0) if has_extra else None
    return y, y_extra, tuple(outs[:n_state]), tuple(outs[n_state:])


def _layernorm_silu(y, g, b):
    mu = jnp.mean(y, axis=-1, keepdims=True)
    yc = y - mu
    yn = yc * jax.lax.rsqrt(jnp.mean(yc * yc, axis=-1, keepdims=True) + EPS) * g + b
    return yn * jax.nn.sigmoid(yn)


def _split_z(z, d_a, d_b):
    a_val = z[:, :d_a]
    a_gate = z[:, d_a:2 * d_a]
    b_gate = z[:, 2 * d_a:2 * d_a + d_b]
    c_gate = z[:, 2 * d_a + d_b:2 * d_a + 2 * d_b]
    v = z[:, 2 * d_a + 2 * d_b:]
    return a_val, a_gate, b_gate, c_gate, v


A_PAD = 32
B_PAD = 8
SKEW_TAPS = CONV_A_WIDTH + SUBLANES - 1


def _mix_ab_tile(x_ref, t, first_step, in_refs, state_refs, scratch_refs, *, tile, tiles_per_seq, d_a, d_b):
    g_ref, win_ref, caw_ref, cab_ref, lng_ref, lnb_ref, cbw_ref, wout_ref = in_refs
    na_ref, nb_ref = state_refs
    exta, extb, ycat, wskew, bgate = scratch_refs

    @pl.when(first_step)
    def _():
        wskew[...] = jnp.zeros(wskew.shape, F32)
        for dd in range(SKEW_TAPS):
            for p in range(SUBLANES):
                if 0 <= dd - p < CONV_A_WIDTH:
                    wskew[dd, p:p + 1, :] = caw_ref[dd - p:dd - p + 1, :]

    @pl.when(t == 0)
    def _():
        exta[0:A_PAD, :] = jnp.zeros((A_PAD, d_a), F32)
        extb[0:B_PAD, :] = jnp.zeros((B_PAD, d_b), F32)

    h = _rms(x_ref[...], g_ref[...]).astype(BF16)
    z = _dot(h, win_ref[...])
    a_val, a_gate, b_gate, c_gate, v = _split_z(z, d_a, d_b)
    exta[A_PAD:A_PAD + tile, :] = a_val * jax.nn.sigmoid(a_gate)
    extb[B_PAD:B_PAD + tile, :] = c_gate * v
    bgate[...] = b_gate

    off_a = A_PAD - (CONV_A_WIDTH - 1)
    off_b = B_PAD - (CONV_B_WIDTH - 1)
    for r0 in range(0, tile, CONV_ROWS):
        rs = slice(r0, r0 + CONV_ROWS)
        for c0 in range(0, d_a, LANES):
            cs = slice(c0, c0 + LANES)
            n_acc = CONV_ROWS // SUBLANES
            accs = [jnp.broadcast_to(cab_ref[:, cs], (SUBLANES, LANES))] * n_acc
            for dd in range(SKEW_TAPS):
                wv = wskew[dd, :, cs]
                for m in range(n_acc):
                    src = r0 + SUBLANES * m + off_a + dd
                    accs[m] = accs[m] + wv * jnp.broadcast_to(exta[src:src + 1, cs], (SUBLANES, LANES))
            for m in range(n_acc):
                ycat[r0 + SUBLANES * m:r0 + SUBLANES * (m + 1), cs] = accs[m]
        ycat[rs, :d_a] = _layernorm_silu(ycat[rs, :d_a], lng_ref[...], lnb_ref[...])
        for c0 in range(0, d_b, LANES):
            cs = slice(c0, c0 + LANES)
            acc = None
            for k in range(CONV_B_WIDTH):
                term = cbw_ref[k:k + 1, cs] * extb[r0 + off_b + k:r0 + off_b + k + CONV_ROWS, cs]
                acc = term if acc is None else acc + term
            ycat[rs, d_a + c0:d_a + c0 + LANES] = bgate[rs, cs] * acc

    out = _dot(ycat[...].astype(BF16), wout_ref[...])

    exta[0:A_PAD, :] = exta[tile:tile + A_PAD, :]
    extb[0:B_PAD, :] = extb[tile:tile + B_PAD, :]

    @pl.when(t == tiles_per_seq - 1)
    def _():
        na_ref[...] = exta[off_a:A_PAD, :]
        nb_ref[...] = extb[off_b:B_PAD, :]

    return x_ref[...] + out


def _mix_ab_prompt(n_seq, seq_len, g, w_in, caw, cab, lng, lnb, cbw, w_out):
    d_a = caw.shape[1]
    d_b = cbw.shape[1]
    tile = min(FFN_TILE, seq_len)
    assert tile >= A_PAD and tile % CONV_ROWS == 0
    row = lambda a: a.reshape(1, -1)
    return _Mixer(
        name="mix_ab",
        tile_fn=functools.partial(_mix_ab_tile, tile=tile, tiles_per_seq=seq_len // tile, d_a=d_a, d_b=d_b),
        ins=(row(g), w_in, caw, row(cab), row(lng), row(lnb), cbw, w_out),
        state_shapes=((n_seq, CONV_A_WIDTH - 1, d_a), (n_seq, CONV_B_WIDTH - 1, d_b)),
        scratch=(pltpu.VMEM((A_PAD + tile, d_a), F32), pltpu.VMEM((B_PAD + tile, d_b), F32),
                 pltpu.VMEM((tile, d_a + d_b), F32), pltpu.VMEM((SKEW_TAPS, SUBLANES, d_a), F32),
                 pltpu.VMEM((tile, d_b), F32)),
        seq_len=seq_len)


def _mix_ab_sample_kernel(x_ref, ha_ref, hb_ref, g_ref, win_ref, caw_ref, cab_ref, lng_ref, lnb_ref,
                          cbw_ref, wout_ref, o_ref, na_ref, nb_ref, ua_s, ub_s, ycat,
                          *, n_seq, t_new, d_a, d_b):
    hist_a = CONV_A_WIDTH - 1
    hist_b = CONV_B_WIDTH - 1

    def rows(t):
        return slice(t * n_seq, (t + 1) * n_seq)

    def ext_a(j):
        return ha_ref[j] if j < hist_a else ua_s[rows(j - hist_a), :]

    def ext_b(j):
        return hb_ref[j] if j < hist_b else ub_s[rows(j - hist_b), :]

    x = x_ref[...]
    h = _rms(x, g_ref[...]).astype(BF16)
    z = _dot(h, win_ref[...])
    a_val, a_gate, b_gate, c_gate, v = _split_z(z, d_a, d_b)
    ua_s[...] = a_val * jax.nn.sigmoid(a_gate)
    ub_s[...] = c_gate * v

    for t in range(t_new):
        acc = jnp.broadcast_to(cab_ref[...], (n_seq, d_a))
        for k in range(CONV_A_WIDTH):
            acc = acc + caw_ref[k:k + 1, :] * ext_a(t + k)
        ycat[rows(t), 0:d_a] = _layernorm_silu(acc, lng_ref[...], lnb_ref[...])
        acc = None
        for k in range(CONV_B_WIDTH):
            term = cbw_ref[k:k + 1, :] * ext_b(t + k)
            acc = term if acc is None else acc + term
        ycat[rows(t), d_a:d_a + d_b] = acc

    ya = ycat[:, :d_a]
    yb = b_gate * ycat[:, d_a:]
    out = _dot(ya.astype(BF16), wout_ref[:d_a, :]) + _dot(yb.astype(BF16), wout_ref[d_a:, :])
    o_ref[...] = x + out

    for j in range(hist_a):
        na_ref[j] = ext_a(j + t_new)
    for j in range(hist_b):
        nb_ref[j] = ext_b(j + t_new)


def _mix_ab_sample(x, hist_a, hist_b, g, w_in, caw, cab, lng, lnb, cbw, w_out):
    n, d = x.shape
    b = hist_a.shape[1]
    d_a = caw.shape[1]
    d_b = cbw.shape[1]
    row = lambda a: a.reshape(1, -1)
    ins = [x, hist_a, hist_b, row(g), w_in, caw, row(cab), row(lng), row(lnb), cbw, w_out]
    return pl.pallas_call(
        functools.partial(_mix_ab_sample_kernel, n_seq=b, t_new=n // b, d_a=d_a, d_b=d_b),
        out_shape=(jax.ShapeDtypeStruct((n, d), F32),
                   jax.ShapeDtypeStruct(hist_a.shape, F32),
                   jax.ShapeDtypeStruct(hist_b.shape, F32)),
        scratch_shapes=[pltpu.VMEM((n, d_a), F32), pltpu.VMEM((n, d_b), F32),
                        pltpu.VMEM((n, d_a + d_b), F32)],
        compiler_params=pltpu.CompilerParams(vmem_limit_bytes=VMEM_LIMIT_BYTES),
        name="mix_ab_sample",
    )(*ins)


P_PAD = 16


def _pool_mix(pooled, pw_ref, pb_ref, ps_ref, group):
    outs = []
    for gi in range(len(POOL_WINDOWS)):
        cs = slice(gi * group, (gi + 1) * group)
        outs.append(_dot(pooled[:, cs].astype(BF16), pw_ref[gi]))
    mixed = jnp.concatenate(outs, axis=-1)
    return (mixed + pb_ref[...]) * ps_ref[...]


def _pool_tile(x_ref, t, first_step, in_refs, state_refs, scratch_refs, *, tile, tiles_per_seq, d, group):
    del first_step
    g_ref, pw_ref, pb_ref, ps_ref = in_refs
    (np_ref,) = state_refs
    exth, pooled_s = scratch_refs

    @pl.when(t == 0)
    def _():
        exth[0:P_PAD, :] = jnp.zeros((P_PAD, d), F32)

    exth[P_PAD:P_PAD + tile, :] = _rms(x_ref[...], g_ref[...])

    pos = (t * tile + jax.lax.broadcasted_iota(jnp.int32, (tile, group), 0)).astype(F32)
    for gi, w in enumerate(POOL_WINDOWS):
        assert w & (w - 1) == 0 and w <= P_PAD
        cs = slice(gi * group, (gi + 1) * group)
        s = exth[:, cs]
        shift = 1
        while shift < w:
            s = s + pltpu.roll(s, shift, 0)
            shift *= 2
        count = jnp.minimum(pos + 1.0, float(w))
        pooled_s[:, cs] = s[P_PAD:, :] / count - exth[P_PAD:P_PAD + tile, cs]

    y = x_ref[...] + _pool_mix(pooled_s[...], pw_ref, pb_ref, ps_ref, group)

    exth[0:P_PAD, :] = exth[tile:tile + P_PAD, :]

    @pl.when(t == tiles_per_seq - 1)
    def _():
        np_ref[...] = exth[P_PAD - POOL_HIST:P_PAD, :]

    return y


def _pool_prompt(n_seq, seq_len, d, g, pw, pb, ps):
    group = pw.shape[-1]
    tile = min(FFN_TILE, seq_len)
    assert tile >= P_PAD
    row = lambda a: a.reshape(1, -1)
    return _Mixer(
        name="pool",
        tile_fn=functools.partial(_pool_tile, tile=tile, tiles_per_seq=seq_len // tile, d=d, group=group),
        ins=(row(g), pw, row(pb), row(ps)),
        state_shapes=((n_seq, POOL_HIST, d),),
        scratch=(pltpu.VMEM((P_PAD + tile, d), F32), pltpu.VMEM((tile, d), F32)),
        seq_len=seq_len)


def _pool_sample_kernel(x_ref, hp_ref, g_ref, pw_ref, pb_ref, ps_ref, o_ref, np_ref, h_s, pooled_s,
                        *, n_seq, t_new, pos0, group):
    def rows(t):
        return slice(t * n_seq, (t + 1) * n_seq)

    def ext(j, cs):
        return hp_ref[j, :, cs] if j < POOL_HIST else h_s[rows(j - POOL_HIST), cs]

    x = x_ref[...]
    h_s[...] = _rms(x, g_ref[...])

    for t in range(t_new):
        for gi, w in enumerate(POOL_WINDOWS):
            cs = slice(gi * group, (gi + 1) * group)
            s = ext(POOL_HIST + t, cs)
            for j in range(1, w):
                s = s + ext(POOL_HIST + t - j, cs)
            count = float(min(pos0 + t + 1, w))
            pooled_s[rows(t), cs] = s / count - ext(POOL_HIST + t, cs)

    o_ref[...] = x + _pool_mix(pooled_s[...], pw_ref, pb_ref, ps_ref, group)

    full = slice(None)
    for j in range(POOL_HIST):
        np_ref[j] = ext(j + t_new, full)


def _pool_sample(x, hist, pos0, g, pw, pb, ps):
    n, d = x.shape
    b = hist.shape[1]
    group = pw.shape[-1]
    row = lambda a: a.reshape(1, -1)
    return pl.pallas_call(
        functools.partial(_pool_sample_kernel, n_seq=b, t_new=n // b, pos0=pos0, group=group),
        out_shape=(jax.ShapeDtypeStruct((n, d), F32), jax.ShapeDtypeStruct(hist.shape, F32)),
        scratch_shapes=[pltpu.VMEM((n, d), F32), pltpu.VMEM((n, d), F32)],
        compiler_params=pltpu.CompilerParams(vmem_limit_bytes=VMEM_LIMIT_BYTES),
        name="pool_sample",
    )(x, hist, row(g), pw, row(pb), row(ps))


def kernel(x_prompt, x_sample, state_conv_a, state_conv_b, state_pool, norm_g, ffn_w_in, ffn_w_out,
           mix_w_in, conv_a_w, conv_a_b, ln_a_g, ln_a_b, conv_b_w, mix_w_out, pool_w, pool_b,
           pool_scale, final_norm_g):
    depth = norm_g.shape[0]
    d = x_prompt.shape[-1]
    bs, ts, _ = x_sample.shape
    step_major = lambda a: jnp.swapaxes(a, 0, 1)

    n_even, n_odd = mix_w_in.shape[0], pool_w.shape[0]
    n_groups, group = pool_w.shape[1], pool_w.shape[2]
    pool_w2 = pool_w.reshape(n_odd, n_groups * group, group)
    mixer_casts = ([(mix_w_in, (e,)) for e in range(n_even)] + [(mix_w_out, (e,)) for e in range(n_even)]
                   + [(pool_w2, (o,)) for o in range(n_odd)])
    ffn_w = {(0, 0): ((ffn_w_in, (0, 0)), (ffn_w_out, (0, 0)))}
    mix_w_in_h, mix_w_out_h, pool_w_h = [], [], []

    def ffn(xp, xs, l, which, final_g=None, mixer=None):
        nxt = (l, 1) if which == 0 else (l + 1, 0)
        cast = [(ffn_w_in, nxt), (ffn_w_out, nxt)] if nxt[0] < depth else []
        if (l, which) == (0, 0):
            cast = cast + mixer_casts
        yp, ys, states, casts = _ffn(xp, norm_g[l, 2 * which], *ffn_w[(l, which)], final_g, cast, mixer,
                                     extra=xs, tile=FFN_TILE if mixer else PLAIN_FFN_TILE)
        if nxt[0] < depth:
            ffn_w[nxt] = casts[:2]
        if (l, which) == (0, 0):
            rest = casts[len(casts) - len(mixer_casts):]
            mix_w_in_h.extend(rest[:n_even])
            mix_w_out_h.extend(rest[n_even:2 * n_even])
            pool_w_h.extend(w.reshape(n_groups, group, group) for w in rest[2 * n_even:])
        return yp, ys, states

    bp, tp, _ = x_prompt.shape
    xp, xs = x_prompt.reshape(bp * tp, d), step_major(x_sample).reshape(ts * bs, d)
    pa, pb, pp, sa, sb, sp = [], [], [], [], [], []
    for l in range(depth):
        final_g = final_norm_g if l == depth - 1 else None
        xp, xs, _ = ffn(xp, xs, l, 0)
        if l % 2 == 0:
            e = l // 2
            wts = (norm_g[l, 1], mix_w_in_h[e], conv_a_w[e], conv_a_b[e], ln_a_g[e], ln_a_b[e],
                   conv_b_w[e], mix_w_out_h[e])
            xs, na, nb = _mix_ab_sample(xs, step_major(state_conv_a[e]), step_major(state_conv_b[e]),
                                        *wts)
            sa.append(step_major(na))
            sb.append(step_major(nb))
            xp, xs, (na, nb) = ffn(xp, xs, l, 1, final_g, _mix_ab_prompt(bp, tp, *wts))
            pa.append(na)
            pb.append(nb)
        else:
            o = l // 2
            wts = (norm_g[l, 1], pool_w_h[o], pool_b[o], pool_scale[o])
            xs, npool = _pool_sample(xs, step_major(state_pool[o]), PAST_LEN, *wts)
            sp.append(step_major(npool))
            xp, xs, (npool,) = ffn(xp, xs, l, 1, final_g, _pool_prompt(bp, tp, d, *wts))
            pp.append(npool)

    xp = xp.reshape(bp, tp, d)
    xs = step_major(xs.reshape(ts, bs, d))
    st = lambda parts: jnp.stack(parts, axis=0)
    return (xp, xs, st(pa), st(pb), st(pp), st(sa), st(sb), st(sp))
```

```python
import functools
from typing import Callable, NamedTuple

import jax
import jax.numpy as jnp
from jax.experimental import pallas as pl
from jax.experimental.pallas import tpu as pltpu

F32 = jnp.float32
BF16 = jnp.bfloat16

EPS = 1e-6
CONV_A_WIDTH = 31
CONV_B_WIDTH = 3
POOL_WINDOWS = (2, 4, 8, 16)
POOL_HIST = max(POOL_WINDOWS) - 1
PAST_LEN = 16384

SUBLANES = 8
BF16_SUBLANES = 16
LANES = 128
VMEM_LIMIT_BYTES = 56 * 1024 * 1024

FFN_TILE = 512
WEIGHT_LOAD_STEPS = 8
FFN_CHUNK = 256
CONV_ROWS = 32


def _rms(x, g):
    ms = jnp.mean(x * x, axis=-1, keepdims=True)
    return x * jax.lax.rsqrt(ms + EPS) * g


def _dot(a, b):
    return jnp.dot(a, b, preferred_element_type=F32)


def _const_spec(shape):
    nd = len(shape)
    return pl.BlockSpec(shape, lambda *_: (0,) * nd, pipeline_mode=pl.Buffered(1))


def _params(sem):
    return pltpu.CompilerParams(dimension_semantics=sem, vmem_limit_bytes=VMEM_LIMIT_BYTES)


def _cast_side(cast_src, cast_dst):
    for src, dst in zip(cast_src, cast_dst):
        dst[...] = src[...].astype(BF16)


def _ffn_tile(x_ref, g_ref, win_ref, wout_ref, fg_ref, *, d_ff, chunk):
    h = _rms(x_ref[...], g_ref[...]).astype(BF16)
    acc = None
    for c in range(d_ff // chunk):
        lo = c * chunk
        gate = _dot(h, win_ref[:, lo:lo + chunk])
        up = _dot(h, win_ref[:, d_ff + lo:d_ff + lo + chunk])
        a = (gate * jax.nn.sigmoid(gate) * up).astype(BF16)
        p = _dot(a, wout_ref[lo:lo + chunk, :])
        acc = p if acc is None else acc + p
    y = x_ref[...] + 0.5 * acc
    if fg_ref is not None:
        y = _rms(y, fg_ref[...])
    return y


class _Mixer(NamedTuple):
    name: str
    tile_fn: Callable
    ins: tuple
    state_shapes: tuple
    scratch: tuple
    seq_len: int


def _ffn_kernel(*refs, d_ff, chunk, final, n_cast, mixer, tiles_per_seq, n_tiles, has_extra, n_load):
    refs = list(refs)
    take = lambda k: [refs.pop(0) for _ in range(k)]
    (x_ref,) = take(1)
    extra_ref = take(1)[0] if has_extra else None
    mix_in = take(len(mixer.ins)) if mixer else []
    g_ref, win_ref, wout_ref = take(3)
    fg_ref = take(1)[0] if final else None
    cast_src = take(n_cast)
    (o_ref,) = take(1)
    extra_o_ref = take(1)[0] if has_extra else None
    state = take(len(mixer.state_shapes)) if mixer else []
    cast_dst = take(n_cast)
    i = pl.program_id(0)
    step = i - n_load
    if n_load:
        win_s, wout_s = take(2)

        @pl.when(i < n_load)
        def _():
            for src, dst in ((win_ref, win_s), (wout_ref, wout_s)):
                rows = src.shape[0]
                dst[pl.ds(pl.multiple_of(i * rows, rows), rows), :] = src[...].astype(BF16)

        win_ref, wout_ref = win_s, wout_s
    mix_scratch = take(len(mixer.scratch)) if mixer else []
    ffn = functools.partial(_ffn_tile, g_ref=g_ref, win_ref=win_ref, wout_ref=wout_ref, fg_ref=fg_ref,
                            d_ff=d_ff, chunk=chunk)
    _cast_side(cast_src, cast_dst)

    def main_step():
        if mixer:
            (xm,) = refs
            xm[...] = mixer.tile_fn(x_ref, jax.lax.rem(step, tiles_per_seq), step == 0, mix_in, state,
                                    mix_scratch)
            o_ref[...] = ffn(xm)
        else:
            o_ref[...] = ffn(x_ref)

    if not (has_extra or n_load):
        main_step()
        return

    pl.when((step >= 0) & (step < n_tiles))(main_step)
    if has_extra:
        @pl.when(step == n_tiles)
        def _():
            extra_o_ref[...] = ffn(extra_ref)


def _cast_specs(w, sel, n_steps, first=0):
    r, c = w.shape[len(sel):]
    n_blocks = n_steps
    while r % n_blocks or (r // n_blocks) % BF16_SUBLANES:
        assert n_blocks % 2 == 0, (r, n_steps)
        n_blocks //= 2
    rep = n_steps // n_blocks
    rows = r // n_blocks
    blk = lambda i: jnp.clip(i - first, 0, n_steps - 1) // rep
    in_spec = pl.BlockSpec((None,) * len(sel) + (rows, c), lambda i: tuple(sel) + (blk(i), 0))
    out_spec = pl.BlockSpec((rows, c), lambda i: (blk(i), 0))
    return in_spec, out_spec, jax.ShapeDtypeStruct((r, c), BF16)


def _ffn(x, g, w_in, w_out, final_g=None, cast=(), mixer=None, extra=None):
    n, d = x.shape
    own_f32 = isinstance(w_in, tuple)
    n_load = WEIGHT_LOAD_STEPS if own_f32 else 0
    w_shapes = [w[0].shape[len(w[1]):] if own_f32 else w.shape for w in (w_in, w_out)]
    d_ff = w_shapes[1][0]
    tile = min(FFN_TILE, n)
    assert n % tile == 0 and d_ff % FFN_CHUNK == 0
    n_tiles = n // tile
    final = final_g is not None
    has_extra = extra is not None
    tiles_per_seq = 1
    if mixer:
        assert mixer.seq_len % tile == 0
        tiles_per_seq = mixer.seq_len // tile
    main_tile = lambda i: jnp.clip(i - n_load, 0, n_tiles - 1)
    x_spec = o_spec = pl.BlockSpec((tile, d), lambda i: (main_tile(i), 0))
    ins, specs = [x], [x_spec]
    out_shapes, out_specs = [jax.ShapeDtypeStruct((n, d), F32)], [o_spec]
    if has_extra:
        ins.append(extra)
        specs.append(pl.BlockSpec(extra.shape, lambda i: (0, 0)))
        out_shapes.append(jax.ShapeDtypeStruct(extra.shape, F32))
        out_specs.append(pl.BlockSpec(extra.shape, lambda i: (0, 0)))
    if mixer:
        ins += list(mixer.ins)
        specs += [_const_spec(a.shape) for a in mixer.ins]
    ins.append(g.reshape(1, d))
    specs.append(_const_spec((1, d)))
    scratch = []
    for w, (r, c) in zip((w_in, w_out), w_shapes):
        if own_f32:
            arr, sel = w
            assert r % n_load == 0 and (r // n_load) % BF16_SUBLANES == 0
            ins.append(arr)
            specs.append(pl.BlockSpec((None,) * len(sel) + (r // n_load, c),
                                      lambda i, sel=sel: tuple(sel) + (jnp.minimum(i, n_load - 1), 0)))
            scratch.append(pltpu.VMEM((r, c), BF16))
        else:
            ins.append(w)
            specs.append(_const_spec((r, c)))
    if final:
        ins.append(final_g.reshape(1, d))
        specs.append(_const_spec((1, d)))
    if mixer:
        for shp in mixer.state_shapes:
            out_shapes.append(jax.ShapeDtypeStruct(shp, F32))
            out_specs.append(pl.BlockSpec((None,) + tuple(shp[1:]),
                                          lambda i: (main_tile(i) // tiles_per_seq, 0, 0)))
        scratch += list(mixer.scratch) + [pltpu.VMEM((tile, d), F32)]
    cast_ins, cast_specs_, cast_out_specs, cast_out_shapes = [], [], [], []
    for w, sel in cast:
        in_spec, out_spec, out_shape = _cast_specs(w, sel, n_tiles, first=n_load)
        cast_ins.append(w)
        cast_specs_.append(in_spec)
        cast_out_specs.append(out_spec)
        cast_out_shapes.append(out_shape)
    n_state = len(mixer.state_shapes) if mixer else 0
    outs = list(pl.pallas_call(
        functools.partial(_ffn_kernel, d_ff=d_ff, chunk=FFN_CHUNK, final=final, n_cast=len(cast),
                          mixer=mixer, tiles_per_seq=tiles_per_seq, n_tiles=n_tiles, has_extra=has_extra,
                          n_load=n_load),
        out_shape=tuple(out_shapes + cast_out_shapes),
        grid=(n_load + n_tiles + (1 if has_extra else 0),),
        in_specs=specs + cast_specs_,
        out_specs=tuple(out_specs + cast_out_specs),
        scratch_shapes=scratch,
        compiler_params=_params(("arbitrary",)),
        name=(mixer.name + "_" if mixer else "") + ("ffn_final" if final else "ffn"),
    )(*ins, *cast_ins))
    y = outs.pop(0)
    y_extra = outs.pop(0) if has_extra else None
    return y, y_extra, tuple(outs[:n_state]), tuple(outs[n_state:])


def _layernorm_silu(y, g, b):
    mu = jnp.mean(y, axis=-1, keepdims=True)
    yc = y - mu
    yn = yc * jax.lax.rsqrt(jnp.mean(yc * yc, axis=-1, keepdims=True) + EPS) * g + b
    return yn * jax.nn.sigmoid(yn)


def _split_z(z, d_a, d_b):
    a_val = z[:, :d_a]
    a_gate = z[:, d_a:2 * d_a]
    b_gate = z[:, 2 * d_a:2 * d_a + d_b]
    c_gate = z[:, 2 * d_a + d_b:2 * d_a + 2 * d_b]
    v = z[:, 2 * d_a + 2 * d_b:]
    return a_val, a_gate, b_gate, c_gate, v


A_PAD = 32
B_PAD = 8
SKEW_TAPS = CONV_A_WIDTH + SUBLANES - 1


def _mix_ab_tile(x_ref, t, first_step, in_refs, state_refs, scratch_refs, *, tile, tiles_per_seq, d_a, d_b):
    g_ref, win_ref, caw_ref, cab_ref, lng_ref, lnb_ref, cbw_ref, wout_ref = in_refs
    na_ref, nb_ref = state_refs
    exta, extb, ycat, wskew, bgate = scratch_refs

    @pl.when(first_step)
    def _():
        wskew[...] = jnp.zeros(wskew.shape, F32)
        for dd in range(SKEW_TAPS):
            for p in range(SUBLANES):
                if 0 <= dd - p < CONV_A_WIDTH:
                    wskew[dd, p:p + 1, :] = caw_ref[dd - p:dd - p + 1, :]

    @pl.when(t == 0)
    def _():
        exta[0:A_PAD, :] = jnp.zeros((A_PAD, d_a), F32)
        extb[0:B_PAD, :] = jnp.zeros((B_PAD, d_b), F32)

    h = _rms(x_ref[...], g_ref[...]).astype(BF16)
    z = _dot(h, win_ref[...])
    a_val, a_gate, b_gate, c_gate, v = _split_z(z, d_a, d_b)
    exta[A_PAD:A_PAD + tile, :] = a_val * jax.nn.sigmoid(a_gate)
    extb[B_PAD:B_PAD + tile, :] = c_gate * v
    bgate[...] = b_gate

    off_a = A_PAD - (CONV_A_WIDTH - 1)
    off_b = B_PAD - (CONV_B_WIDTH - 1)
    for r0 in range(0, tile, CONV_ROWS):
        rs = slice(r0, r0 + CONV_ROWS)
        for c0 in range(0, d_a, LANES):
            cs = slice(c0, c0 + LANES)
            n_acc = CONV_ROWS // SUBLANES
            accs = [jnp.broadcast_to(cab_ref[:, cs], (SUBLANES, LANES))] * n_acc
            for dd in range(SKEW_TAPS):
                wv = wskew[dd, :, cs]
                for m in range(n_acc):
                    src = r0 + SUBLANES * m + off_a + dd
                    accs[m] = accs[m] + wv * jnp.broadcast_to(exta[src:src + 1, cs], (SUBLANES, LANES))
            for m in range(n_acc):
                ycat[r0 + SUBLANES * m:r0 + SUBLANES * (m + 1), cs] = accs[m]
        ycat[rs, :d_a] = _layernorm_silu(ycat[rs, :d_a], lng_ref[...], lnb_ref[...])
        for c0 in range(0, d_b, LANES):
            cs = slice(c0, c0 + LANES)
            acc = None
            for k in range(CONV_B_WIDTH):
                term = cbw_ref[k:k + 1, cs] * extb[r0 + off_b + k:r0 + off_b + k + CONV_ROWS, cs]
                acc = term if acc is None else acc + term
            ycat[rs, d_a + c0:d_a + c0 + LANES] = bgate[rs, cs] * acc

    out = _dot(ycat[...].astype(BF16), wout_ref[...])

    exta[0:A_PAD, :] = exta[tile:tile + A_PAD, :]
    extb[0:B_PAD, :] = extb[tile:tile + B_PAD, :]

    @pl.when(t == tiles_per_seq - 1)
    def _():
        na_ref[...] = exta[off_a:A_PAD, :]
        nb_ref[...] = extb[off_b:B_PAD, :]

    return x_ref[...] + out


def _mix_ab_prompt(n_seq, seq_len, g, w_in, caw, cab, lng, lnb, cbw, w_out):
    d_a = caw.shape[1]
    d_b = cbw.shape[1]
    tile = min(FFN_TILE, seq_len)
    assert tile >= A_PAD and tile % CONV_ROWS == 0
    row = lambda a: a.reshape(1, -1)
    return _Mixer(
        name="mix_ab",
        tile_fn=functools.partial(_mix_ab_tile, tile=tile, tiles_per_seq=seq_len // tile, d_a=d_a, d_b=d_b),
        ins=(row(g), w_in, caw, row(cab), row(lng), row(lnb), cbw, w_out),
        state_shapes=((n_seq, CONV_A_WIDTH - 1, d_a), (n_seq, CONV_B_WIDTH - 1, d_b)),
        scratch=(pltpu.VMEM((A_PAD + tile, d_a), F32), pltpu.VMEM((B_PAD + tile, d_b), F32),
                 pltpu.VMEM((tile, d_a + d_b), F32), pltpu.VMEM((SKEW_TAPS, SUBLANES, d_a), F32),
                 pltpu.VMEM((tile, d_b), F32)),
        seq_len=seq_len)


def _mix_ab_sample_kernel(x_ref, ha_ref, hb_ref, g_ref, win_ref, caw_ref, cab_ref, lng_ref, lnb_ref,
                          cbw_ref, wout_ref, o_ref, na_ref, nb_ref, ua_s, ub_s, ycat,
                          *, n_seq, t_new, d_a, d_b):
    hist_a = CONV_A_WIDTH - 1
    hist_b = CONV_B_WIDTH - 1

    def rows(t):
        return slice(t * n_seq, (t + 1) * n_seq)

    def ext_a(j):
        return ha_ref[j] if j < hist_a else ua_s[rows(j - hist_a), :]

    def ext_b(j):
        return hb_ref[j] if j < hist_b else ub_s[rows(j - hist_b), :]

    x = x_ref[...]
    h = _rms(x, g_ref[...]).astype(BF16)
    z = _dot(h, win_ref[...])
    a_val, a_gate, b_gate, c_gate, v = _split_z(z, d_a, d_b)
    ua_s[...] = a_val * jax.nn.sigmoid(a_gate)
    ub_s[...] = c_gate * v

    for t in range(t_new):
        acc = jnp.broadcast_to(cab_ref[...], (n_seq, d_a))
        for k in range(CONV_A_WIDTH):
            acc = acc + caw_ref[k:k + 1, :] * ext_a(t + k)
        ycat[rows(t), 0:d_a] = _layernorm_silu(acc, lng_ref[...], lnb_ref[...])
        acc = None
        for k in range(CONV_B_WIDTH):
            term = cbw_ref[k:k + 1, :] * ext_b(t + k)
            acc = term if acc is None else acc + term
        ycat[rows(t), d_a:d_a + d_b] = acc

    ya = ycat[:, :d_a]
    yb = b_gate * ycat[:, d_a:]
    out = _dot(ya.astype(BF16), wout_ref[:d_a, :]) + _dot(yb.astype(BF16), wout_ref[d_a:, :])
    o_ref[...] = x + out

    for j in range(hist_a):
        na_ref[j] = ext_a(j + t_new)
    for j in range(hist_b):
        nb_ref[j] = ext_b(j + t_new)


def _mix_ab_sample(x, hist_a, hist_b, g, w_in, caw, cab, lng, lnb, cbw, w_out):
    n, d = x.shape
    b = hist_a.shape[1]
    d_a = caw.shape[1]
    d_b = cbw.shape[1]
    row = lambda a: a.reshape(1, -1)
    ins = [x, hist_a, hist_b, row(g), w_in, caw, row(cab), row(lng), row(lnb), cbw, w_out]
    return pl.pallas_call(
        functools.partial(_mix_ab_sample_kernel, n_seq=b, t_new=n // b, d_a=d_a, d_b=d_b),
        out_shape=(jax.ShapeDtypeStruct((n, d), F32),
                   jax.ShapeDtypeStruct(hist_a.shape, F32),
                   jax.ShapeDtypeStruct(hist_b.shape, F32)),
        scratch_shapes=[pltpu.VMEM((n, d_a), F32), pltpu.VMEM((n, d_b), F32),
                        pltpu.VMEM((n, d_a + d_b), F32)],
        compiler_params=pltpu.CompilerParams(vmem_limit_bytes=VMEM_LIMIT_BYTES),
        name="mix_ab_sample",
    )(*ins)


P_PAD = 16


def _pool_mix(pooled, pw_ref, pb_ref, ps_ref, group):
    outs = []
    for gi in range(len(POOL_WINDOWS)):
        cs = slice(gi * group, (gi + 1) * group)
        outs.append(_dot(pooled[:, cs].astype(BF16), pw_ref[gi]))
    mixed = jnp.concatenate(outs, axis=-1)
    return (mixed + pb_ref[...]) * ps_ref[...]


def _pool_tile(x_ref, t, first_step, in_refs, state_refs, scratch_refs, *, tile, tiles_per_seq, d, group):
    del first_step
    g_ref, pw_ref, pb_ref, ps_ref = in_refs
    (np_ref,) = state_refs
    exth, pooled_s = scratch_refs

    @pl.when(t == 0)
    def _():
        exth[0:P_PAD, :] = jnp.zeros((P_PAD, d), F32)

    exth[P_PAD:P_PAD + tile, :] = _rms(x_ref[...], g_ref[...])

    pos = (t * tile + jax.lax.broadcasted_iota(jnp.int32, (tile, group), 0)).astype(F32)
    for gi, w in enumerate(POOL_WINDOWS):
        assert w & (w - 1) == 0 and w <= P_PAD
        cs = slice(gi * group, (gi + 1) * group)
        s = exth[:, cs]
        shift = 1
        while shift < w:
            s = s + pltpu.roll(s, shift, 0)
            shift *= 2
        count = jnp.minimum(pos + 1.0, float(w))
        pooled_s[:, cs] = s[P_PAD:, :] / count - exth[P_PAD:P_PAD + tile, cs]

    y = x_ref[...] + _pool_mix(pooled_s[...], pw_ref, pb_ref, ps_ref, group)

    exth[0:P_PAD, :] = exth[tile:tile + P_PAD, :]

    @pl.when(t == tiles_per_seq - 1)
    def _():
        np_ref[...] = exth[P_PAD - POOL_HIST:P_PAD, :]

    return y


def _pool_prompt(n_seq, seq_len, d, g, pw, pb, ps):
    group = pw.shape[-1]
    tile = min(FFN_TILE, seq_len)
    assert tile >= P_PAD
    row = lambda a: a.reshape(1, -1)
    return _Mixer(
        name="pool",
        tile_fn=functools.partial(_pool_tile, tile=tile, tiles_per_seq=seq_len // tile, d=d, group=group),
        ins=(row(g), pw, row(pb), row(ps)),
        state_shapes=((n_seq, POOL_HIST, d),),
        scratch=(pltpu.VMEM((P_PAD + tile, d), F32), pltpu.VMEM((tile, d), F32)),
        seq_len=seq_len)


def _pool_sample_kernel(x_ref, hp_ref, g_ref, pw_ref, pb_ref, ps_ref, o_ref, np_ref, h_s, pooled_s,
                        *, n_seq, t_new, pos0, group):
    def rows(t):
        return slice(t * n_seq, (t + 1) * n_seq)

    def ext(j, cs):
        return hp_ref[j, :, cs] if j < POOL_HIST else h_s[rows(j - POOL_HIST), cs]

    x = x_ref[...]
    h_s[...] = _rms(x, g_ref[...])

    for t in range(t_new):
        for gi, w in enumerate(POOL_WINDOWS):
            cs = slice(gi * group, (gi + 1) * group)
            s = ext(POOL_HIST + t, cs)
            for j in range(1, w):
                s = s + ext(POOL_HIST + t - j, cs)
            count = float(min(pos0 + t + 1, w))
            pooled_s[rows(t), cs] = s / count - ext(POOL_HIST + t, cs)

    o_ref[...] = x + _pool_mix(pooled_s[...], pw_ref, pb_ref, ps_ref, group)

    full = slice(None)
    for j in range(POOL_HIST):
        np_ref[j] = ext(j + t_new, full)


def _pool_sample(x, hist, pos0, g, pw, pb, ps):
    n, d = x.shape
    b = hist.shape[1]
    group = pw.shape[-1]
    row = lambda a: a.reshape(1, -1)
    return pl.pallas_call(
        functools.partial(_pool_sample_kernel, n_seq=b, t_new=n // b, pos0=pos0, group=group),
        out_shape=(jax.ShapeDtypeStruct((n, d), F32), jax.ShapeDtypeStruct(hist.shape, F32)),
        scratch_shapes=[pltpu.VMEM((n, d), F32), pltpu.VMEM((n, d), F32)],
        compiler_params=pltpu.CompilerParams(vmem_limit_bytes=VMEM_LIMIT_BYTES),
        name="pool_sample",
    )(x, hist, row(g), pw, row(pb), row(ps))


def kernel(x_prompt, x_sample, state_conv_a, state_conv_b, state_pool, norm_g, ffn_w_in, ffn_w_out,
           mix_w_in, conv_a_w, conv_a_b, ln_a_g, ln_a_b, conv_b_w, mix_w_out, pool_w, pool_b,
           pool_scale, final_norm_g):
    depth = norm_g.shape[0]
    d = x_prompt.shape[-1]
    bs, ts, _ = x_sample.shape
    step_major = lambda a: jnp.swapaxes(a, 0, 1)

    n_even, n_odd = mix_w_in.shape[0], pool_w.shape[0]
    n_groups, group = pool_w.shape[1], pool_w.shape[2]
    pool_w2 = pool_w.reshape(n_odd, n_groups * group, group)
    mixer_casts = ([(mix_w_in, (e,)) for e in range(n_even)] + [(mix_w_out, (e,)) for e in range(n_even)]
                   + [(pool_w2, (o,)) for o in range(n_odd)])
    ffn_w = {(0, 0): ((ffn_w_in, (0, 0)), (ffn_w_out, (0, 0)))}
    mix_w_in_h, mix_w_out_h, pool_w_h = [], [], []

    def ffn(xp, xs, l, which, final_g=None, mixer=None):
        nxt = (l, 1) if which == 0 else (l + 1, 0)
        cast = [(ffn_w_in, nxt), (ffn_w_out, nxt)] if nxt[0] < depth else []
        if (l, which) == (0, 0):
            cast = cast + mixer_casts
        yp, ys, states, casts = _ffn(xp, norm_g[l, 2 * which], *ffn_w[(l, which)], final_g, cast, mixer,
                                     extra=xs)
        if nxt[0] < depth:
            ffn_w[nxt] = casts[:2]
        if (l, which) == (0, 0):
            rest = casts[len(casts) - len(mixer_casts):]
            mix_w_in_h.extend(rest[:n_even])
            mix_w_out_h.extend(rest[n_even:2 * n_even])
            pool_w_h.extend(w.reshape(n_groups, group, group) for w in rest[2 * n_even:])
        return yp, ys, states

    bp, tp, _ = x_prompt.shape
    xp, xs = x_prompt.reshape(bp * tp, d), step_major(x_sample).reshape(ts * bs, d)
    pa, pb, pp, sa, sb, sp = [], [], [], [], [], []
    for l in range(depth):
        final_g = final_norm_g if l == depth - 1 else None
        xp, xs, _ = ffn(xp, xs, l, 0)
        if l % 2 == 0:
            e = l // 2
            wts = (norm_g[l, 1], mix_w_in_h[e], conv_a_w[e], conv_a_b[e], ln_a_g[e], ln_a_b[e],
                   conv_b_w[e], mix_w_out_h[e])
            xs, na, nb = _mix_ab_sample(xs, step_major(state_conv_a[e]), step_major(state_conv_b[e]),
                                        *wts)
            sa.append(step_major(na))
            sb.append(step_major(nb))
            xp, xs, (na, nb) = ffn(xp, xs, l, 1, final_g, _mix_ab_prompt(bp, tp, *wts))
            pa.append(na)
            pb.append(nb)
        else:
            o = l // 2
            wts = (norm_g[l, 1], pool_w_h[o], pool_b[o], pool_scale[o])
            xs, npool = _pool_sample(xs, step_major(state_pool[o]), PAST_LEN, *wts)
            sp.append(step_major(npool))
            xp, xs, (npool,) = ffn(xp, xs, l, 1, final_g, _pool_prompt(bp, tp, d, *wts))
            pp.append(npool)

    xp = xp.reshape(bp, tp, d)
    xs = step_major(xs.reshape(ts, bs, d))
    st = lambda parts: jnp.stack(parts, axis=0)
    return (xp, xs, st(pa), st(pb), st(pp), st(sa), st(sb), st(sp))
```

```python
import functools
from typing import Callable, NamedTuple

import jax
import jax.numpy as jnp
from jax.experimental import pallas as pl
from jax.experimental.pallas import tpu as pltpu

F32 = jnp.float32
BF16 = jnp.bfloat16

EPS = 1e-6
CONV_A_WIDTH = 31
CONV_B_WIDTH = 3
POOL_WINDOWS = (2, 4, 8, 16)
POOL_HIST = max(POOL_WINDOWS) - 1
PAST_LEN = 16384

SUBLANES = 8
BF16_SUBLANES = 16
LANES = 128
VMEM_LIMIT_BYTES = 56 * 1024 * 1024

FFN_TILE = 512
TILES_PER_STEP = 2
WEIGHT_LOAD_STEPS = 8
FFN_CHUNK = 256
CONV_ROWS = 32


def _rms(x, g):
    ms = jnp.mean(x * x, axis=-1, keepdims=True)
    return x * jax.lax.rsqrt(ms + EPS) * g


def _dot(a, b):
    return jnp.dot(a, b, preferred_element_type=F32)


def _const_spec(shape):
    nd = len(shape)
    return pl.BlockSpec(shape, lambda *_: (0,) * nd, pipeline_mode=pl.Buffered(1))


def _params(sem):
    return pltpu.CompilerParams(dimension_semantics=sem, vmem_limit_bytes=VMEM_LIMIT_BYTES)


def _cast_side(cast_src, cast_dst):
    for src, dst in zip(cast_src, cast_dst):
        dst[...] = src[...].astype(BF16)


def _ffn_tile(x_ref, g_ref, win_ref, wout_ref, fg_ref, *, d_ff, chunk):
    h = _rms(x_ref[...], g_ref[...]).astype(BF16)
    acc = None
    for c in range(d_ff // chunk):
        lo = c * chunk
        gate = _dot(h, win_ref[:, lo:lo + chunk])
        up = _dot(h, win_ref[:, d_ff + lo:d_ff + lo + chunk])
        a = (gate * jax.nn.sigmoid(gate) * up).astype(BF16)
        p = _dot(a, wout_ref[lo:lo + chunk, :])
        acc = p if acc is None else acc + p
    y = x_ref[...] + 0.5 * acc
    if fg_ref is not None:
        y = _rms(y, fg_ref[...])
    return y


class _Mixer(NamedTuple):
    name: str
    tile_fn: Callable
    ins: tuple
    state_shapes: tuple
    scratch: tuple
    seq_len: int
    tiles_per_step: int


def _ffn_kernel(*refs, d_ff, chunk, final, n_cast, mixer, tiles_per_seq, tiles_per_step, n_tiles,
                has_extra, n_load):
    refs = list(refs)
    take = lambda k: [refs.pop(0) for _ in range(k)]
    (x_ref,) = take(1)
    extra_ref = take(1)[0] if has_extra else None
    mix_in = take(len(mixer.ins)) if mixer else []
    g_ref, win_ref, wout_ref = take(3)
    fg_ref = take(1)[0] if final else None
    cast_src = take(n_cast)
    (o_ref,) = take(1)
    extra_o_ref = take(1)[0] if has_extra else None
    state = take(len(mixer.state_shapes)) if mixer else []
    cast_dst = take(n_cast)
    i = pl.program_id(0)
    step = i - n_load
    if n_load:
        win_s, wout_s = take(2)

        @pl.when(i < n_load)
        def _():
            for src, dst in ((win_ref, win_s), (wout_ref, wout_s)):
                rows = src.shape[0]
                dst[pl.ds(pl.multiple_of(i * rows, rows), rows), :] = src[...].astype(BF16)

        win_ref, wout_ref = win_s, wout_s
    mix_scratch = take(len(mixer.scratch)) if mixer else []
    ffn = functools.partial(_ffn_tile, g_ref=g_ref, win_ref=win_ref, wout_ref=wout_ref, fg_ref=fg_ref,
                            d_ff=d_ff, chunk=chunk)
    _cast_side(cast_src, cast_dst)

    def one_tile(sub, carry):
        rows = pl.ds(pl.multiple_of(sub * FFN_TILE, FFN_TILE), FFN_TILE)
        xv, ov = x_ref.at[rows], o_ref.at[rows]
        if mixer:
            (xm,) = refs
            k = step * tiles_per_step + sub
            xm[...] = mixer.tile_fn(xv, jax.lax.rem(k, tiles_per_seq), k == 0, mix_in, state, mix_scratch)
            ov[...] = ffn(xm)
        else:
            ov[...] = ffn(xv)
        return carry

    def main_step():
        jax.lax.fori_loop(0, tiles_per_step, one_tile, 0)

    if not (has_extra or n_load):
        main_step()
        return

    pl.when((step >= 0) & (step < n_tiles))(main_step)
    if has_extra:
        @pl.when(step == n_tiles)
        def _():
            extra_o_ref[...] = ffn(extra_ref)


def _cast_specs(w, sel, n_steps, first=0):
    r, c = w.shape[len(sel):]
    n_blocks = n_steps
    while r % n_blocks or (r // n_blocks) % BF16_SUBLANES:
        assert n_blocks % 2 == 0, (r, n_steps)
        n_blocks //= 2
    rep = n_steps // n_blocks
    rows = r // n_blocks
    blk = lambda i: jnp.clip(i - first, 0, n_steps - 1) // rep
    in_spec = pl.BlockSpec((None,) * len(sel) + (rows, c), lambda i: tuple(sel) + (blk(i), 0))
    out_spec = pl.BlockSpec((rows, c), lambda i: (blk(i), 0))
    return in_spec, out_spec, jax.ShapeDtypeStruct((r, c), BF16)


def _ffn(x, g, w_in, w_out, final_g=None, cast=(), mixer=None, extra=None):
    n, d = x.shape
    own_f32 = isinstance(w_in, tuple)
    n_load = WEIGHT_LOAD_STEPS if own_f32 else 0
    w_shapes = [w[0].shape[len(w[1]):] if own_f32 else w.shape for w in (w_in, w_out)]
    d_ff = w_shapes[1][0]
    tile = FFN_TILE
    tiles_per_step = mixer.tiles_per_step if mixer else TILES_PER_STEP
    block = tile * tiles_per_step
    assert n % block == 0 and d_ff % FFN_CHUNK == 0
    n_tiles = n // block
    final = final_g is not None
    has_extra = extra is not None
    tiles_per_seq = 1
    if mixer:
        assert mixer.seq_len % tile == 0
        tiles_per_seq = mixer.seq_len // tile
    main_tile = lambda i: jnp.clip(i - n_load, 0, n_tiles - 1)
    x_spec = o_spec = pl.BlockSpec((block, d), lambda i: (main_tile(i), 0))
    ins, specs = [x], [x_spec]
    out_shapes, out_specs = [jax.ShapeDtypeStruct((n, d), F32)], [o_spec]
    if has_extra:
        ins.append(extra)
        specs.append(pl.BlockSpec(extra.shape, lambda i: (0, 0)))
        out_shapes.append(jax.ShapeDtypeStruct(extra.shape, F32))
        out_specs.append(pl.BlockSpec(extra.shape, lambda i: (0, 0)))
    if mixer:
        ins += list(mixer.ins)
        specs += [_const_spec(a.shape) for a in mixer.ins]
    ins.append(g.reshape(1, d))
    specs.append(_const_spec((1, d)))
    scratch = []
    for w, (r, c) in zip((w_in, w_out), w_shapes):
        if own_f32:
            arr, sel = w
            assert r % n_load == 0 and (r // n_load) % BF16_SUBLANES == 0
            ins.append(arr)
            specs.append(pl.BlockSpec((None,) * len(sel) + (r // n_load, c),
                                      lambda i, sel=sel: tuple(sel) + (jnp.minimum(i, n_load - 1), 0)))
            scratch.append(pltpu.VMEM((r, c), BF16))
        else:
            ins.append(w)
            specs.append(_const_spec((r, c)))
    if final:
        ins.append(final_g.reshape(1, d))
        specs.append(_const_spec((1, d)))
    if mixer:
        for shp in mixer.state_shapes:
            out_shapes.append(jax.ShapeDtypeStruct(shp, F32))
            out_specs.append(pl.BlockSpec((None,) + tuple(shp[1:]),
                                          lambda i: (main_tile(i) * tiles_per_step // tiles_per_seq, 0, 0)))
        scratch += list(mixer.scratch) + [pltpu.VMEM((tile, d), F32)]
    cast_ins, cast_specs_, cast_out_specs, cast_out_shapes = [], [], [], []
    for w, sel in cast:
        in_spec, out_spec, out_shape = _cast_specs(w, sel, n_tiles, first=n_load)
        cast_ins.append(w)
        cast_specs_.append(in_spec)
        cast_out_specs.append(out_spec)
        cast_out_shapes.append(out_shape)
    n_state = len(mixer.state_shapes) if mixer else 0
    outs = list(pl.pallas_call(
        functools.partial(_ffn_kernel, d_ff=d_ff, chunk=FFN_CHUNK, final=final, n_cast=len(cast),
                          mixer=mixer, tiles_per_seq=tiles_per_seq, tiles_per_step=tiles_per_step,
                          n_tiles=n_tiles, has_extra=has_extra,
                          n_load=n_load),
        out_shape=tuple(out_shapes + cast_out_shapes),
        grid=(n_load + n_tiles + (1 if has_extra else 0),),
        in_specs=specs + cast_specs_,
        out_specs=tuple(out_specs + cast_out_specs),
        scratch_shapes=scratch,
        compiler_params=_params(("arbitrary",)),
        name=(mixer.name + "_" if mixer else "") + ("ffn_final" if final else "ffn"),
    )(*ins, *cast_ins))
    y = outs.pop(0)
    y_extra = outs.pop(0) if has_extra else None
    return y, y_extra, tuple(outs[:n_state]), tuple(outs[n_state:])


def _layernorm_silu(y, g, b):
    mu = jnp.mean(y, axis=-1, keepdims=True)
    yc = y - mu
    yn = yc * jax.lax.rsqrt(jnp.mean(yc * yc, axis=-1, keepdims=True) + EPS) * g + b
    return yn * jax.nn.sigmoid(yn)


def _split_z(z, d_a, d_b):
    a_val = z[:, :d_a]
    a_gate = z[:, d_a:2 * d_a]
    b_gate = z[:, 2 * d_a:2 * d_a + d_b]
    c_gate = z[:, 2 * d_a + d_b:2 * d_a + 2 * d_b]
    v = z[:, 2 * d_a + 2 * d_b:]
    return a_val, a_gate, b_gate, c_gate, v


A_PAD = 32
B_PAD = 8
SKEW_TAPS = CONV_A_WIDTH + SUBLANES - 1


def _mix_ab_tile(x_ref, t, first_step, in_refs, state_refs, scratch_refs, *, tile, tiles_per_seq, d_a, d_b):
    g_ref, win_ref, caw_ref, cab_ref, lng_ref, lnb_ref, cbw_ref, wout_ref = in_refs
    na_ref, nb_ref = state_refs
    exta, extb, ycat, wskew, bgate = scratch_refs

    @pl.when(first_step)
    def _():
        wskew[...] = jnp.zeros(wskew.shape, F32)
        for dd in range(SKEW_TAPS):
            for p in range(SUBLANES):
                if 0 <= dd - p < CONV_A_WIDTH:
                    wskew[dd, p:p + 1, :] = caw_ref[dd - p:dd - p + 1, :]

    @pl.when(t == 0)
    def _():
        exta[0:A_PAD, :] = jnp.zeros((A_PAD, d_a), F32)
        extb[0:B_PAD, :] = jnp.zeros((B_PAD, d_b), F32)

    h = _rms(x_ref[...], g_ref[...]).astype(BF16)
    z = _dot(h, win_ref[...])
    a_val, a_gate, b_gate, c_gate, v = _split_z(z, d_a, d_b)
    exta[A_PAD:A_PAD + tile, :] = a_val * jax.nn.sigmoid(a_gate)
    extb[B_PAD:B_PAD + tile, :] = c_gate * v
    bgate[...] = b_gate

    off_a = A_PAD - (CONV_A_WIDTH - 1)
    off_b = B_PAD - (CONV_B_WIDTH - 1)
    for r0 in range(0, tile, CONV_ROWS):
        rs = slice(r0, r0 + CONV_ROWS)
        for c0 in range(0, d_a, LANES):
            cs = slice(c0, c0 + LANES)
            n_acc = CONV_ROWS // SUBLANES
            accs = [jnp.broadcast_to(cab_ref[:, cs], (SUBLANES, LANES))] * n_acc
            for dd in range(SKEW_TAPS):
                wv = wskew[dd, :, cs]
                for m in range(n_acc):
                    src = r0 + SUBLANES * m + off_a + dd
                    accs[m] = accs[m] + wv * jnp.broadcast_to(exta[src:src + 1, cs], (SUBLANES, LANES))
            for m in range(n_acc):
                ycat[r0 + SUBLANES * m:r0 + SUBLANES * (m + 1), cs] = accs[m]
        ycat[rs, :d_a] = _layernorm_silu(ycat[rs, :d_a], lng_ref[...], lnb_ref[...])
        for c0 in range(0, d_b, LANES):
            cs = slice(c0, c0 + LANES)
            acc = None
            for k in range(CONV_B_WIDTH):
                term = cbw_ref[k:k + 1, cs] * extb[r0 + off_b + k:r0 + off_b + k + CONV_ROWS, cs]
                acc = term if acc is None else acc + term
            ycat[rs, d_a + c0:d_a + c0 + LANES] = bgate[rs, cs] * acc

    out = _dot(ycat[...].astype(BF16), wout_ref[...])

    exta[0:A_PAD, :] = exta[tile:tile + A_PAD, :]
    extb[0:B_PAD, :] = extb[tile:tile + B_PAD, :]

    @pl.when(t == tiles_per_seq - 1)
    def _():
        na_ref[...] = exta[off_a:A_PAD, :]
        nb_ref[...] = extb[off_b:B_PAD, :]

    return x_ref[...] + out


def _mix_ab_prompt(n_seq, seq_len, g, w_in, caw, cab, lng, lnb, cbw, w_out):
    d_a = caw.shape[1]
    d_b = cbw.shape[1]
    tile = min(FFN_TILE, seq_len)
    assert tile >= A_PAD and tile % CONV_ROWS == 0
    row = lambda a: a.reshape(1, -1)
    return _Mixer(
        name="mix_ab",
        tile_fn=functools.partial(_mix_ab_tile, tile=tile, tiles_per_seq=seq_len // tile, d_a=d_a, d_b=d_b),
        ins=(row(g), w_in, caw, row(cab), row(lng), row(lnb), cbw, w_out),
        state_shapes=((n_seq, CONV_A_WIDTH - 1, d_a), (n_seq, CONV_B_WIDTH - 1, d_b)),
        scratch=(pltpu.VMEM((A_PAD + tile, d_a), F32), pltpu.VMEM((B_PAD + tile, d_b), F32),
                 pltpu.VMEM((tile, d_a + d_b), F32), pltpu.VMEM((SKEW_TAPS, SUBLANES, d_a), F32),
                 pltpu.VMEM((tile, d_b), F32)),
        seq_len=seq_len, tiles_per_step=1)


def _mix_ab_sample_kernel(x_ref, ha_ref, hb_ref, g_ref, win_ref, caw_ref, cab_ref, lng_ref, lnb_ref,
                          cbw_ref, wout_ref, o_ref, na_ref, nb_ref, ua_s, ub_s, ycat,
                          *, n_seq, t_new, d_a, d_b):
    hist_a = CONV_A_WIDTH - 1
    hist_b = CONV_B_WIDTH - 1

    def rows(t):
        return slice(t * n_seq, (t + 1) * n_seq)

    def ext_a(j):
        return ha_ref[j] if j < hist_a else ua_s[rows(j - hist_a), :]

    def ext_b(j):
        return hb_ref[j] if j < hist_b else ub_s[rows(j - hist_b), :]

    x = x_ref[...]
    h = _rms(x, g_ref[...]).astype(BF16)
    z = _dot(h, win_ref[...])
    a_val, a_gate, b_gate, c_gate, v = _split_z(z, d_a, d_b)
    ua_s[...] = a_val * jax.nn.sigmoid(a_gate)
    ub_s[...] = c_gate * v

    for t in range(t_new):
        acc = jnp.broadcast_to(cab_ref[...], (n_seq, d_a))
        for k in range(CONV_A_WIDTH):
            acc = acc + caw_ref[k:k + 1, :] * ext_a(t + k)
        ycat[rows(t), 0:d_a] = _layernorm_silu(acc, lng_ref[...], lnb_ref[...])
        acc = None
        for k in range(CONV_B_WIDTH):
            term = cbw_ref[k:k + 1, :] * ext_b(t + k)
            acc = term if acc is None else acc + term
        ycat[rows(t), d_a:d_a + d_b] = acc

    ya = ycat[:, :d_a]
    yb = b_gate * ycat[:, d_a:]
    out = _dot(ya.astype(BF16), wout_ref[:d_a, :]) + _dot(yb.astype(BF16), wout_ref[d_a:, :])
    o_ref[...] = x + out

    for j in range(hist_a):
        na_ref[j] = ext_a(j + t_new)
    for j in range(hist_b):
        nb_ref[j] = ext_b(j + t_new)


def _mix_ab_sample(x, hist_a, hist_b, g, w_in, caw, cab, lng, lnb, cbw, w_out):
    n, d = x.shape
    b = hist_a.shape[1]
    d_a = caw.shape[1]
    d_b = cbw.shape[1]
    row = lambda a: a.reshape(1, -1)
    ins = [x, hist_a, hist_b, row(g), w_in, caw, row(cab), row(lng), row(lnb), cbw, w_out]
    return pl.pallas_call(
        functools.partial(_mix_ab_sample_kernel, n_seq=b, t_new=n // b, d_a=d_a, d_b=d_b),
        out_shape=(jax.ShapeDtypeStruct((n, d), F32),
                   jax.ShapeDtypeStruct(hist_a.shape, F32),
                   jax.ShapeDtypeStruct(hist_b.shape, F32)),
        scratch_shapes=[pltpu.VMEM((n, d_a), F32), pltpu.VMEM((n, d_b), F32),
                        pltpu.VMEM((n, d_a + d_b), F32)],
        compiler_params=pltpu.CompilerParams(vmem_limit_bytes=VMEM_LIMIT_BYTES),
        name="mix_ab_sample",
    )(*ins)


P_PAD = 16


def _pool_mix(pooled, pw_ref, pb_ref, ps_ref, group):
    outs = []
    for gi in range(len(POOL_WINDOWS)):
        cs = slice(gi * group, (gi + 1) * group)
        outs.append(_dot(pooled[:, cs].astype(BF16), pw_ref[gi]))
    mixed = jnp.concatenate(outs, axis=-1)
    return (mixed + pb_ref[...]) * ps_ref[...]


def _pool_tile(x_ref, t, first_step, in_refs, state_refs, scratch_refs, *, tile, tiles_per_seq, d, group):
    del first_step
    g_ref, pw_ref, pb_ref, ps_ref = in_refs
    (np_ref,) = state_refs
    exth, pooled_s = scratch_refs

    @pl.when(t == 0)
    def _():
        exth[0:P_PAD, :] = jnp.zeros((P_PAD, d), F32)

    exth[P_PAD:P_PAD + tile, :] = _rms(x_ref[...], g_ref[...])

    pos = (t * tile + jax.lax.broadcasted_iota(jnp.int32, (tile, group), 0)).astype(F32)
    for gi, w in enumerate(POOL_WINDOWS):
        assert w & (w - 1) == 0 and w <= P_PAD
        cs = slice(gi * group, (gi + 1) * group)
        s = exth[:, cs]
        shift = 1
        while shift < w:
            s = s + pltpu.roll(s, shift, 0)
            shift *= 2
        count = jnp.minimum(pos + 1.0, float(w))
        pooled_s[:, cs] = s[P_PAD:, :] / count - exth[P_PAD:P_PAD + tile, cs]

    y = x_ref[...] + _pool_mix(pooled_s[...], pw_ref, pb_ref, ps_ref, group)

    exth[0:P_PAD, :] = exth[tile:tile + P_PAD, :]

    @pl.when(t == tiles_per_seq - 1)
    def _():
        np_ref[...] = exth[P_PAD - POOL_HIST:P_PAD, :]

    return y


def _pool_prompt(n_seq, seq_len, d, g, pw, pb, ps):
    group = pw.shape[-1]
    tile = min(FFN_TILE, seq_len)
    assert tile >= P_PAD
    row = lambda a: a.reshape(1, -1)
    return _Mixer(
        name="pool",
        tile_fn=functools.partial(_pool_tile, tile=tile, tiles_per_seq=seq_len // tile, d=d, group=group),
        ins=(row(g), pw, row(pb), row(ps)),
        state_shapes=((n_seq, POOL_HIST, d),),
        scratch=(pltpu.VMEM((P_PAD + tile, d), F32), pltpu.VMEM((tile, d), F32)),
        seq_len=seq_len, tiles_per_step=TILES_PER_STEP)


def _pool_sample_kernel(x_ref, hp_ref, g_ref, pw_ref, pb_ref, ps_ref, o_ref, np_ref, h_s, pooled_s,
                        *, n_seq, t_new, pos0, group):
    def rows(t):
        return slice(t * n_seq, (t + 1) * n_seq)

    def ext(j, cs):
        return hp_ref[j, :, cs] if j < POOL_HIST else h_s[rows(j - POOL_HIST), cs]

    x = x_ref[...]
    h_s[...] = _rms(x, g_ref[...])

    for t in range(t_new):
        for gi, w in enumerate(POOL_WINDOWS):
            cs = slice(gi * group, (gi + 1) * group)
            s = ext(POOL_HIST + t, cs)
            for j in range(1, w):
                s = s + ext(POOL_HIST + t - j, cs)
            count = float(min(pos0 + t + 1, w))
            pooled_s[rows(t), cs] = s / count - ext(POOL_HIST + t, cs)

    o_ref[...] = x + _pool_mix(pooled_s[...], pw_ref, pb_ref, ps_ref, group)

    full = slice(None)
    for j in range(POOL_HIST):
        np_ref[j] = ext(j + t_new, full)


def _pool_sample(x, hist, pos0, g, pw, pb, ps):
    n, d = x.shape
    b = hist.shape[1]
    group = pw.shape[-1]
    row = lambda a: a.reshape(1, -1)
    return pl.pallas_call(
        functools.partial(_pool_sample_kernel, n_seq=b, t_new=n // b, pos0=pos0, group=group),
        out_shape=(jax.ShapeDtypeStruct((n, d), F32), jax.ShapeDtypeStruct(hist.shape, F32)),
        scratch_shapes=[pltpu.VMEM((n, d), F32), pltpu.VMEM((n, d), F32)],
        compiler_params=pltpu.CompilerParams(vmem_limit_bytes=VMEM_LIMIT_BYTES),
        name="pool_sample",
    )(x, hist, row(g), pw, row(pb), row(ps))


def kernel(x_prompt, x_sample, state_conv_a, state_conv_b, state_pool, norm_g, ffn_w_in, ffn_w_out,
           mix_w_in, conv_a_w, conv_a_b, ln_a_g, ln_a_b, conv_b_w, mix_w_out, pool_w, pool_b,
           pool_scale, final_norm_g):
    depth = norm_g.shape[0]
    d = x_prompt.shape[-1]
    bs, ts, _ = x_sample.shape
    step_major = lambda a: jnp.swapaxes(a, 0, 1)

    n_even, n_odd = mix_w_in.shape[0], pool_w.shape[0]
    n_groups, group = pool_w.shape[1], pool_w.shape[2]
    pool_w2 = pool_w.reshape(n_odd, n_groups * group, group)
    mixer_casts = ([(mix_w_in, (e,)) for e in range(n_even)] + [(mix_w_out, (e,)) for e in range(n_even)]
                   + [(pool_w2, (o,)) for o in range(n_odd)])
    ffn_w = {(0, 0): ((ffn_w_in, (0, 0)), (ffn_w_out, (0, 0)))}
    mix_w_in_h, mix_w_out_h, pool_w_h = [], [], []

    def ffn(xp, xs, l, which, final_g=None, mixer=None):
        nxt = (l, 1) if which == 0 else (l + 1, 0)
        cast = [(ffn_w_in, nxt), (ffn_w_out, nxt)] if nxt[0] < depth else []
        if (l, which) == (0, 0):
            cast = cast + mixer_casts
        yp, ys, states, casts = _ffn(xp, norm_g[l, 2 * which], *ffn_w[(l, which)], final_g, cast, mixer,
                                     extra=xs)
        if nxt[0] < depth:
            ffn_w[nxt] = casts[:2]
        if (l, which) == (0, 0):
            rest = casts[len(casts) - len(mixer_casts):]
            mix_w_in_h.extend(rest[:n_even])
            mix_w_out_h.extend(rest[n_even:2 * n_even])
            pool_w_h.extend(w.reshape(n_groups, group, group) for w in rest[2 * n_even:])
        return yp, ys, states

    bp, tp, _ = x_prompt.shape
    xp, xs = x_prompt.reshape(bp * tp, d), step_major(x_sample).reshape(ts * bs, d)
    pa, pb, pp, sa, sb, sp = [], [], [], [], [], []
    for l in range(depth):
        final_g = final_norm_g if l == depth - 1 else None
        xp, xs, _ = ffn(xp, xs, l, 0)
        if l % 2 == 0:
            e = l // 2
            wts = (norm_g[l, 1], mix_w_in_h[e], conv_a_w[e], conv_a_b[e], ln_a_g[e], ln_a_b[e],
                   conv_b_w[e], mix_w_out_h[e])
            xs, na, nb = _mix_ab_sample(xs, step_major(state_conv_a[e]), step_major(state_conv_b[e]),
                                        *wts)
            sa.append(step_major(na))
            sb.append(step_major(nb))
            xp, xs, (na, nb) = ffn(xp, xs, l, 1, final_g, _mix_ab_prompt(bp, tp, *wts))
            pa.append(na)
            pb.append(nb)
        else:
            o = l // 2
            wts = (norm_g[l, 1], pool_w_h[o], pool_b[o], pool_scale[o])
            xs, npool = _pool_sample(xs, step_major(state_pool[o]), PAST_LEN, *wts)
            sp.append(step_major(npool))
            xp, xs, (npool,) = ffn(xp, xs, l, 1, final_g, _pool_prompt(bp, tp, d, *wts))
            pp.append(npool)

    xp = xp.reshape(bp, tp, d)
    xs = step_major(xs.reshape(ts, bs, d))
    st = lambda parts: jnp.stack(parts, axis=0)
    return (xp, xs, st(pa), st(pb), st(pp), st(sa), st(sb), st(sp))
```

```python
import functools
from typing import Callable, NamedTuple

import jax
import jax.numpy as jnp
from jax.experimental import pallas as pl
from jax.experimental.pallas import tpu as pltpu

F32 = jnp.float32
BF16 = jnp.bfloat16

EPS = 1e-6
CONV_A_WIDTH = 31
CONV_B_WIDTH = 3
POOL_WINDOWS = (2, 4, 8, 16)
POOL_HIST = max(POOL_WINDOWS) - 1
PAST_LEN = 16384

SUBLANES = 8
BF16_SUBLANES = 16
LANES = 128
VMEM_LIMIT_BYTES = 56 * 1024 * 1024

FFN_TILE = 512
WEIGHT_LOAD_STEPS = 8
FFN_CHUNK = 256
CONV_ROWS = 32


def _rms(x, g):
    ms = jnp.mean(x * x, axis=-1, keepdims=True)
    return x * jax.lax.rsqrt(ms + EPS) * g


def _dot(a, b):
    return jnp.dot(a, b, preferred_element_type=F32)


def _const_spec(shape):
    nd = len(shape)
    return pl.BlockSpec(shape, lambda *_: (0,) * nd, pipeline_mode=pl.Buffered(1))


def _params(sem):
    return pltpu.CompilerParams(dimension_semantics=sem, vmem_limit_bytes=VMEM_LIMIT_BYTES)


def _cast_side(cast_src, cast_dst):
    for src, dst in zip(cast_src, cast_dst):
        dst[...] = src[...].astype(BF16)


def _ffn_tile(x_ref, g_ref, win_ref, wout_ref, fg_ref, act_ref, *, d_ff, chunk):
    h = _rms(x_ref[...], g_ref[...]).astype(BF16)
    for c in range(d_ff // chunk):
        lo = c * chunk
        gate = _dot(h, win_ref[:, lo:lo + chunk])
        up = _dot(h, win_ref[:, d_ff + lo:d_ff + lo + chunk])
        act_ref[:, lo:lo + chunk] = (gate * jax.nn.sigmoid(gate) * up).astype(BF16)
    acc = _dot(act_ref[...], wout_ref[...])
    y = x_ref[...] + 0.5 * acc
    if fg_ref is not None:
        y = _rms(y, fg_ref[...])
    return y


class _Mixer(NamedTuple):
    name: str
    tile_fn: Callable
    ins: tuple
    state_shapes: tuple
    scratch: tuple
    seq_len: int


def _ffn_kernel(*refs, d_ff, chunk, final, n_cast, mixer, tiles_per_seq, n_tiles, has_extra, n_load):
    refs = list(refs)
    take = lambda k: [refs.pop(0) for _ in range(k)]
    (x_ref,) = take(1)
    extra_ref = take(1)[0] if has_extra else None
    mix_in = take(len(mixer.ins)) if mixer else []
    g_ref, win_ref, wout_ref = take(3)
    fg_ref = take(1)[0] if final else None
    cast_src = take(n_cast)
    (o_ref,) = take(1)
    extra_o_ref = take(1)[0] if has_extra else None
    state = take(len(mixer.state_shapes)) if mixer else []
    cast_dst = take(n_cast)
    i = pl.program_id(0)
    step = i - n_load
    if n_load:
        win_s, wout_s = take(2)

        @pl.when(i < n_load)
        def _():
            for src, dst in ((win_ref, win_s), (wout_ref, wout_s)):
                rows = src.shape[0]
                dst[pl.ds(pl.multiple_of(i * rows, rows), rows), :] = src[...].astype(BF16)

        win_ref, wout_ref = win_s, wout_s
    (act_ref,) = take(1)
    mix_scratch = take(len(mixer.scratch)) if mixer else []
    ffn = functools.partial(_ffn_tile, g_ref=g_ref, win_ref=win_ref, wout_ref=wout_ref, fg_ref=fg_ref,
                            act_ref=act_ref, d_ff=d_ff, chunk=chunk)
    _cast_side(cast_src, cast_dst)

    def main_step():
        if mixer:
            (xm,) = refs
            xm[...] = mixer.tile_fn(x_ref, jax.lax.rem(step, tiles_per_seq), step == 0, mix_in, state,
                                    mix_scratch)
            o_ref[...] = ffn(xm)
        else:
            o_ref[...] = ffn(x_ref)

    if not (has_extra or n_load):
        main_step()
        return

    pl.when((step >= 0) & (step < n_tiles))(main_step)
    if has_extra:
        @pl.when(step == n_tiles)
        def _():
            extra_o_ref[...] = ffn(extra_ref)


def _cast_specs(w, sel, n_steps, first=0):
    r, c = w.shape[len(sel):]
    n_blocks = n_steps
    while r % n_blocks or (r // n_blocks) % BF16_SUBLANES:
        assert n_blocks % 2 == 0, (r, n_steps)
        n_blocks //= 2
    rep = n_steps // n_blocks
    rows = r // n_blocks
    blk = lambda i: jnp.clip(i - first, 0, n_steps - 1) // rep
    in_spec = pl.BlockSpec((None,) * len(sel) + (rows, c), lambda i: tuple(sel) + (blk(i), 0))
    out_spec = pl.BlockSpec((rows, c), lambda i: (blk(i), 0))
    return in_spec, out_spec, jax.ShapeDtypeStruct((r, c), BF16)


def _ffn(x, g, w_in, w_out, final_g=None, cast=(), mixer=None, extra=None):
    n, d = x.shape
    own_f32 = isinstance(w_in, tuple)
    n_load = WEIGHT_LOAD_STEPS if own_f32 else 0
    w_shapes = [w[0].shape[len(w[1]):] if own_f32 else w.shape for w in (w_in, w_out)]
    d_ff = w_shapes[1][0]
    tile = min(FFN_TILE, n)
    assert n % tile == 0 and d_ff % FFN_CHUNK == 0
    n_tiles = n // tile
    final = final_g is not None
    has_extra = extra is not None
    tiles_per_seq = 1
    if mixer:
        assert mixer.seq_len % tile == 0
        tiles_per_seq = mixer.seq_len // tile
    main_tile = lambda i: jnp.clip(i - n_load, 0, n_tiles - 1)
    x_spec = o_spec = pl.BlockSpec((tile, d), lambda i: (main_tile(i), 0))
    ins, specs = [x], [x_spec]
    out_shapes, out_specs = [jax.ShapeDtypeStruct((n, d), F32)], [o_spec]
    if has_extra:
        ins.append(extra)
        specs.append(pl.BlockSpec(extra.shape, lambda i: (0, 0)))
        out_shapes.append(jax.ShapeDtypeStruct(extra.shape, F32))
        out_specs.append(pl.BlockSpec(extra.shape, lambda i: (0, 0)))
    if mixer:
        ins += list(mixer.ins)
        specs += [_const_spec(a.shape) for a in mixer.ins]
    ins.append(g.reshape(1, d))
    specs.append(_const_spec((1, d)))
    scratch = []
    for w, (r, c) in zip((w_in, w_out), w_shapes):
        if own_f32:
            arr, sel = w
            assert r % n_load == 0 and (r // n_load) % BF16_SUBLANES == 0
            ins.append(arr)
            specs.append(pl.BlockSpec((None,) * len(sel) + (r // n_load, c),
                                      lambda i, sel=sel: tuple(sel) + (jnp.minimum(i, n_load - 1), 0)))
            scratch.append(pltpu.VMEM((r, c), BF16))
        else:
            ins.append(w)
            specs.append(_const_spec((r, c)))
    if final:
        ins.append(final_g.reshape(1, d))
        specs.append(_const_spec((1, d)))
    assert not has_extra or extra.shape[0] == tile
    scratch.append(pltpu.VMEM((tile, d_ff), BF16))
    if mixer:
        for shp in mixer.state_shapes:
            out_shapes.append(jax.ShapeDtypeStruct(shp, F32))
            out_specs.append(pl.BlockSpec((None,) + tuple(shp[1:]),
                                          lambda i: (main_tile(i) // tiles_per_seq, 0, 0)))
        scratch += list(mixer.scratch) + [pltpu.VMEM((tile, d), F32)]
    cast_ins, cast_specs_, cast_out_specs, cast_out_shapes = [], [], [], []
    for w, sel in cast:
        in_spec, out_spec, out_shape = _cast_specs(w, sel, n_tiles, first=n_load)
        cast_ins.append(w)
        cast_specs_.append(in_spec)
        cast_out_specs.append(out_spec)
        cast_out_shapes.append(out_shape)
    n_state = len(mixer.state_shapes) if mixer else 0
    outs = list(pl.pallas_call(
        functools.partial(_ffn_kernel, d_ff=d_ff, chunk=FFN_CHUNK, final=final, n_cast=len(cast),
                          mixer=mixer, tiles_per_seq=tiles_per_seq, n_tiles=n_tiles, has_extra=has_extra,
                          n_load=n_load),
        out_shape=tuple(out_shapes + cast_out_shapes),
        grid=(n_load + n_tiles + (1 if has_extra else 0),),
        in_specs=specs + cast_specs_,
        out_specs=tuple(out_specs + cast_out_specs),
        scratch_shapes=scratch,
        compiler_params=_params(("arbitrary",)),
        name=(mixer.name + "_" if mixer else "") + ("ffn_final" if final else "ffn"),
    )(*ins, *cast_ins))
    y = outs.pop(0)
    y_extra = outs.pop(0) if has_extra else None
    return y, y_extra, tuple(outs[:n_state]), tuple(outs[n_state:])


def _layernorm_silu(y, g, b):
    mu = jnp.mean(y, axis=-1, keepdims=True)
    yc = y - mu
    yn = yc * jax.lax.rsqrt(jnp.mean(yc * yc, axis=-1, keepdims=True) + EPS) * g + b
    return yn * jax.nn.sigmoid(yn)


def _split_z(z, d_a, d_b):
    a_val = z[:, :d_a]
    a_gate = z[:, d_a:2 * d_a]
    b_gate = z[:, 2 * d_a:2 * d_a + d_b]
    c_gate = z[:, 2 * d_a + d_b:2 * d_a + 2 * d_b]
    v = z[:, 2 * d_a + 2 * d_b:]
    return a_val, a_gate, b_gate, c_gate, v


A_PAD = 32
B_PAD = 8
SKEW_TAPS = CONV_A_WIDTH + SUBLANES - 1


def _mix_ab_tile(x_ref, t, first_step, in_refs, state_refs, scratch_refs, *, tile, tiles_per_seq, d_a, d_b):
    g_ref, win_ref, caw_ref, cab_ref, lng_ref, lnb_ref, cbw_ref, wout_ref = in_refs
    na_ref, nb_ref = state_refs
    exta, extb, ycat, wskew, bgate = scratch_refs

    @pl.when(first_step)
    def _():
        wskew[...] = jnp.zeros(wskew.shape, F32)
        for dd in range(SKEW_TAPS):
            for p in range(SUBLANES):
                if 0 <= dd - p < CONV_A_WIDTH:
                    wskew[dd, p:p + 1, :] = caw_ref[dd - p:dd - p + 1, :]

    @pl.when(t == 0)
    def _():
        exta[0:A_PAD, :] = jnp.zeros((A_PAD, d_a), F32)
        extb[0:B_PAD, :] = jnp.zeros((B_PAD, d_b), F32)

    h = _rms(x_ref[...], g_ref[...]).astype(BF16)
    z = _dot(h, win_ref[...])
    a_val, a_gate, b_gate, c_gate, v = _split_z(z, d_a, d_b)
    exta[A_PAD:A_PAD + tile, :] = a_val * jax.nn.sigmoid(a_gate)
    extb[B_PAD:B_PAD + tile, :] = c_gate * v
    bgate[...] = b_gate

    off_a = A_PAD - (CONV_A_WIDTH - 1)
    off_b = B_PAD - (CONV_B_WIDTH - 1)
    for r0 in range(0, tile, CONV_ROWS):
        rs = slice(r0, r0 + CONV_ROWS)
        for c0 in range(0, d_a, LANES):
            cs = slice(c0, c0 + LANES)
            n_acc = CONV_ROWS // SUBLANES
            accs = [jnp.broadcast_to(cab_ref[:, cs], (SUBLANES, LANES))] * n_acc
            for dd in range(SKEW_TAPS):
                wv = wskew[dd, :, cs]
                for m in range(n_acc):
                    src = r0 + SUBLANES * m + off_a + dd
                    accs[m] = accs[m] + wv * jnp.broadcast_to(exta[src:src + 1, cs], (SUBLANES, LANES))
            for m in range(n_acc):
                ycat[r0 + SUBLANES * m:r0 + SUBLANES * (m + 1), cs] = accs[m]
        ycat[rs, :d_a] = _layernorm_silu(ycat[rs, :d_a], lng_ref[...], lnb_ref[...])
        for c0 in range(0, d_b, LANES):
            cs = slice(c0, c0 + LANES)
            acc = None
            for k in range(CONV_B_WIDTH):
                term = cbw_ref[k:k + 1, cs] * extb[r0 + off_b + k:r0 + off_b + k + CONV_ROWS, cs]
                acc = term if acc is None else acc + term
            ycat[rs, d_a + c0:d_a + c0 + LANES] = bgate[rs, cs] * acc

    out = _dot(ycat[...].astype(BF16), wout_ref[...])

    exta[0:A_PAD, :] = exta[tile:tile + A_PAD, :]
    extb[0:B_PAD, :] = extb[tile:tile + B_PAD, :]

    @pl.when(t == tiles_per_seq - 1)
    def _():
        na_ref[...] = exta[off_a:A_PAD, :]
        nb_ref[...] = extb[off_b:B_PAD, :]

    return x_ref[...] + out


def _mix_ab_prompt(n_seq, seq_len, g, w_in, caw, cab, lng, lnb, cbw, w_out):
    d_a = caw.shape[1]
    d_b = cbw.shape[1]
    tile = min(FFN_TILE, seq_len)
    assert tile >= A_PAD and tile % CONV_ROWS == 0
    row = lambda a: a.reshape(1, -1)
    return _Mixer(
        name="mix_ab",
        tile_fn=functools.partial(_mix_ab_tile, tile=tile, tiles_per_seq=seq_len // tile, d_a=d_a, d_b=d_b),
        ins=(row(g), w_in, caw, row(cab), row(lng), row(lnb), cbw, w_out),
        state_shapes=((n_seq, CONV_A_WIDTH - 1, d_a), (n_seq, CONV_B_WIDTH - 1, d_b)),
        scratch=(pltpu.VMEM((A_PAD + tile, d_a), F32), pltpu.VMEM((B_PAD + tile, d_b), F32),
                 pltpu.VMEM((tile, d_a + d_b), F32), pltpu.VMEM((SKEW_TAPS, SUBLANES, d_a), F32),
                 pltpu.VMEM((tile, d_b), F32)),
        seq_len=seq_len)


def _mix_ab_sample_kernel(x_ref, ha_ref, hb_ref, g_ref, win_ref, caw_ref, cab_ref, lng_ref, lnb_ref,
                          cbw_ref, wout_ref, o_ref, na_ref, nb_ref, ua_s, ub_s, ycat,
                          *, n_seq, t_new, d_a, d_b):
    hist_a = CONV_A_WIDTH - 1
    hist_b = CONV_B_WIDTH - 1

    def rows(t):
        return slice(t * n_seq, (t + 1) * n_seq)

    def ext_a(j):
        return ha_ref[j] if j < hist_a else ua_s[rows(j - hist_a), :]

    def ext_b(j):
        return hb_ref[j] if j < hist_b else ub_s[rows(j - hist_b), :]

    x = x_ref[...]
    h = _rms(x, g_ref[...]).astype(BF16)
    z = _dot(h, win_ref[...])
    a_val, a_gate, b_gate, c_gate, v = _split_z(z, d_a, d_b)
    ua_s[...] = a_val * jax.nn.sigmoid(a_gate)
    ub_s[...] = c_gate * v

    for t in range(t_new):
        acc = jnp.broadcast_to(cab_ref[...], (n_seq, d_a))
        for k in range(CONV_A_WIDTH):
            acc = acc + caw_ref[k:k + 1, :] * ext_a(t + k)
        ycat[rows(t), 0:d_a] = _layernorm_silu(acc, lng_ref[...], lnb_ref[...])
        acc = None
        for k in range(CONV_B_WIDTH):
            term = cbw_ref[k:k + 1, :] * ext_b(t + k)
            acc = term if acc is None else acc + term
        ycat[rows(t), d_a:d_a + d_b] = acc

    ya = ycat[:, :d_a]
    yb = b_gate * ycat[:, d_a:]
    out = _dot(ya.astype(BF16), wout_ref[:d_a, :]) + _dot(yb.astype(BF16), wout_ref[d_a:, :])
    o_ref[...] = x + out

    for j in range(hist_a):
        na_ref[j] = ext_a(j + t_new)
    for j in range(hist_b):
        nb_ref[j] = ext_b(j + t_new)


def _mix_ab_sample(x, hist_a, hist_b, g, w_in, caw, cab, lng, lnb, cbw, w_out):
    n, d = x.shape
    b = hist_a.shape[1]
    d_a = caw.shape[1]
    d_b = cbw.shape[1]
    row = lambda a: a.reshape(1, -1)
    ins = [x, hist_a, hist_b, row(g), w_in, caw, row(cab), row(lng), row(lnb), cbw, w_out]
    return pl.pallas_call(
        functools.partial(_mix_ab_sample_kernel, n_seq=b, t_new=n // b, d_a=d_a, d_b=d_b),
        out_shape=(jax.ShapeDtypeStruct((n, d), F32),
                   jax.ShapeDtypeStruct(hist_a.shape, F32),
                   jax.ShapeDtypeStruct(hist_b.shape, F32)),
        scratch_shapes=[pltpu.VMEM((n, d_a), F32), pltpu.VMEM((n, d_b), F32),
                        pltpu.VMEM((n, d_a + d_b), F32)],
        compiler_params=pltpu.CompilerParams(vmem_limit_bytes=VMEM_LIMIT_BYTES),
        name="mix_ab_sample",
    )(*ins)


P_PAD = 16


def _pool_mix(pooled, pw_ref, pb_ref, ps_ref, group):
    outs = []
    for gi in range(len(POOL_WINDOWS)):
        cs = slice(gi * group, (gi + 1) * group)
        outs.append(_dot(pooled[:, cs].astype(BF16), pw_ref[gi]))
    mixed = jnp.concatenate(outs, axis=-1)
    return (mixed + pb_ref[...]) * ps_ref[...]


def _pool_tile(x_ref, t, first_step, in_refs, state_refs, scratch_refs, *, tile, tiles_per_seq, d, group):
    del first_step
    g_ref, pw_ref, pb_ref, ps_ref = in_refs
    (np_ref,) = state_refs
    exth, pooled_s = scratch_refs

    @pl.when(t == 0)
    def _():
        exth[0:P_PAD, :] = jnp.zeros((P_PAD, d), F32)

    exth[P_PAD:P_PAD + tile, :] = _rms(x_ref[...], g_ref[...])

    pos = (t * tile + jax.lax.broadcasted_iota(jnp.int32, (tile, group), 0)).astype(F32)
    for gi, w in enumerate(POOL_WINDOWS):
        assert w & (w - 1) == 0 and w <= P_PAD
        cs = slice(gi * group, (gi + 1) * group)
        s = exth[:, cs]
        shift = 1
        while shift < w:
            s = s + pltpu.roll(s, shift, 0)
            shift *= 2
        count = jnp.minimum(pos + 1.0, float(w))
        pooled_s[:, cs] = s[P_PAD:, :] / count - exth[P_PAD:P_PAD + tile, cs]

    y = x_ref[...] + _pool_mix(pooled_s[...], pw_ref, pb_ref, ps_ref, group)

    exth[0:P_PAD, :] = exth[tile:tile + P_PAD, :]

    @pl.when(t == tiles_per_seq - 1)
    def _():
        np_ref[...] = exth[P_PAD - POOL_HIST:P_PAD, :]

    return y


def _pool_prompt(n_seq, seq_len, d, g, pw, pb, ps):
    group = pw.shape[-1]
    tile = min(FFN_TILE, seq_len)
    assert tile >= P_PAD
    row = lambda a: a.reshape(1, -1)
    return _Mixer(
        name="pool",
        tile_fn=functools.partial(_pool_tile, tile=tile, tiles_per_seq=seq_len // tile, d=d, group=group),
        ins=(row(g), pw, row(pb), row(ps)),
        state_shapes=((n_seq, POOL_HIST, d),),
        scratch=(pltpu.VMEM((P_PAD + tile, d), F32), pltpu.VMEM((tile, d), F32)),
        seq_len=seq_len)


def _pool_sample_kernel(x_ref, hp_ref, g_ref, pw_ref, pb_ref, ps_ref, o_ref, np_ref, h_s, pooled_s,
                        *, n_seq, t_new, pos0, group):
    def rows(t):
        return slice(t * n_seq, (t + 1) * n_seq)

    def ext(j, cs):
        return hp_ref[j, :, cs] if j < POOL_HIST else h_s[rows(j - POOL_HIST), cs]

    x = x_ref[...]
    h_s[...] = _rms(x, g_ref[...])

    for t in range(t_new):
        for gi, w in enumerate(POOL_WINDOWS):
            cs = slice(gi * group, (gi + 1) * group)
            s = ext(POOL_HIST + t, cs)
            for j in range(1, w):
                s = s + ext(POOL_HIST + t - j, cs)
            count = float(min(pos0 + t + 1, w))
            pooled_s[rows(t), cs] = s / count - ext(POOL_HIST + t, cs)

    o_ref[...] = x + _pool_mix(pooled_s[...], pw_ref, pb_ref, ps_ref, group)

    full = slice(None)
    for j in range(POOL_HIST):
        np_ref[j] = ext(j + t_new, full)


def _pool_sample(x, hist, pos0, g, pw, pb, ps):
    n, d = x.shape
    b = hist.shape[1]
    group = pw.shape[-1]
    row = lambda a: a.reshape(1, -1)
    return pl.pallas_call(
        functools.partial(_pool_sample_kernel, n_seq=b, t_new=n // b, pos0=pos0, group=group),
        out_shape=(jax.ShapeDtypeStruct((n, d), F32), jax.ShapeDtypeStruct(hist.shape, F32)),
        scratch_shapes=[pltpu.VMEM((n, d), F32), pltpu.VMEM((n, d), F32)],
        compiler_params=pltpu.CompilerParams(vmem_limit_bytes=VMEM_LIMIT_BYTES),
        name="pool_sample",
    )(x, hist, row(g), pw, row(pb), row(ps))


def kernel(x_prompt, x_sample, state_conv_a, state_conv_b, state_pool, norm_g, ffn_w_in, ffn_w_out,
           mix_w_in, conv_a_w, conv_a_b, ln_a_g, ln_a_b, conv_b_w, mix_w_out, pool_w, pool_b,
           pool_scale, final_norm_g):
    depth = norm_g.shape[0]
    d = x_prompt.shape[-1]
    bs, ts, _ = x_sample.shape
    step_major = lambda a: jnp.swapaxes(a, 0, 1)

    n_even, n_odd = mix_w_in.shape[0], pool_w.shape[0]
    n_groups, group = pool_w.shape[1], pool_w.shape[2]
    pool_w2 = pool_w.reshape(n_odd, n_groups * group, group)
    mixer_casts = ([(mix_w_in, (e,)) for e in range(n_even)] + [(mix_w_out, (e,)) for e in range(n_even)]
                   + [(pool_w2, (o,)) for o in range(n_odd)])
    ffn_w = {(0, 0): ((ffn_w_in, (0, 0)), (ffn_w_out, (0, 0)))}
    mix_w_in_h, mix_w_out_h, pool_w_h = [], [], []

    def ffn(xp, xs, l, which, final_g=None, mixer=None):
        nxt = (l, 1) if which == 0 else (l + 1, 0)
        cast = [(ffn_w_in, nxt), (ffn_w_out, nxt)] if nxt[0] < depth else []
        if (l, which) == (0, 0):
            cast = cast + mixer_casts
        yp, ys, states, casts = _ffn(xp, norm_g[l, 2 * which], *ffn_w[(l, which)], final_g, cast, mixer,
                                     extra=xs)
        if nxt[0] < depth:
            ffn_w[nxt] = casts[:2]
        if (l, which) == (0, 0):
            rest = casts[len(casts) - len(mixer_casts):]
            mix_w_in_h.extend(rest[:n_even])
            mix_w_out_h.extend(rest[n_even:2 * n_even])
            pool_w_h.extend(w.reshape(n_groups, group, group) for w in rest[2 * n_even:])
        return yp, ys, states

    bp, tp, _ = x_prompt.shape
    xp, xs = x_prompt.reshape(bp * tp, d), step_major(x_sample).reshape(ts * bs, d)
    pa, pb, pp, sa, sb, sp = [], [], [], [], [], []
    for l in range(depth):
        final_g = final_norm_g if l == depth - 1 else None
        xp, xs, _ = ffn(xp, xs, l, 0)
        if l % 2 == 0:
            e = l // 2
            wts = (norm_g[l, 1], mix_w_in_h[e], conv_a_w[e], conv_a_b[e], ln_a_g[e], ln_a_b[e],
                   conv_b_w[e], mix_w_out_h[e])
            xs, na, nb = _mix_ab_sample(xs, step_major(state_conv_a[e]), step_major(state_conv_b[e]),
                                        *wts)
            sa.append(step_major(na))
            sb.append(step_major(nb))
            xp, xs, (na, nb) = ffn(xp, xs, l, 1, final_g, _mix_ab_prompt(bp, tp, *wts))
            pa.append(na)
            pb.append(nb)
        else:
            o = l // 2
            wts = (norm_g[l, 1], pool_w_h[o], pool_b[o], pool_scale[o])
            xs, npool = _pool_sample(xs, step_major(state_pool[o]), PAST_LEN, *wts)
            sp.append(step_major(npool))
            xp, xs, (npool,) = ffn(xp, xs, l, 1, final_g, _pool_prompt(bp, tp, d, *wts))
            pp.append(npool)

    xp = xp.reshape(bp, tp, d)
    xs = step_major(xs.reshape(ts, bs, d))
    st = lambda parts: jnp.stack(parts, axis=0)
    return (xp, xs, st(pa), st(pb), st(pp), st(sa), st(sb), st(sp))
```

```python
import functools
from typing import Callable, NamedTuple

import jax
import jax.numpy as jnp
from jax.experimental import pallas as pl
from jax.experimental.pallas import tpu as pltpu

F32 = jnp.float32
BF16 = jnp.bfloat16

EPS = 1e-6
CONV_A_WIDTH = 31
CONV_B_WIDTH = 3
POOL_WINDOWS = (2, 4, 8, 16)
POOL_HIST = max(POOL_WINDOWS) - 1
PAST_LEN = 16384

SUBLANES = 8
BF16_SUBLANES = 16
LANES = 128
VMEM_LIMIT_BYTES = 56 * 1024 * 1024

FFN_TILE = 512
WEIGHT_LOAD_STEPS = 8
FFN_CHUNK = 256
CONV_ROWS = 32


def _rms(x, g):
    ms = jnp.mean(x * x, axis=-1, keepdims=True)
    return x * jax.lax.rsqrt(ms + EPS) * g


def _dot(a, b):
    return jnp.dot(a, b, preferred_element_type=F32)


def _const_spec(shape):
    nd = len(shape)
    return pl.BlockSpec(shape, lambda *_: (0,) * nd, pipeline_mode=pl.Buffered(1))


def _params(sem):
    return pltpu.CompilerParams(dimension_semantics=sem, vmem_limit_bytes=VMEM_LIMIT_BYTES)


def _cast_side(cast_src, cast_dst):
    for src, dst in zip(cast_src, cast_dst):
        dst[...] = src[...].astype(BF16)


def _ffn_tile(x_ref, g_ref, win_ref, wout_ref, fg_ref, act_ref, *, d_ff, chunk):
    h = _rms(x_ref[...], g_ref[...]).astype(BF16)
    for c in range(d_ff // chunk):
        lo = c * chunk
        gate = _dot(h, win_ref[:, lo:lo + chunk])
        up = _dot(h, win_ref[:, d_ff + lo:d_ff + lo + chunk])
        act_ref[:, lo:lo + chunk] = (gate * jax.nn.sigmoid(gate) * up).astype(BF16)
    acc = _dot(act_ref[...], wout_ref[...])
    y = x_ref[...] + 0.5 * acc
    if fg_ref is not None:
        y = _rms(y, fg_ref[...])
    return y


class _Mixer(NamedTuple):
    name: str
    tile_fn: Callable
    ins: tuple
    state_shapes: tuple
    scratch: tuple
    seq_len: int


def _ffn_kernel(*refs, d_ff, chunk, final, n_cast, mixer, tiles_per_seq, n_tiles, has_extra, n_load):
    refs = list(refs)
    take = lambda k: [refs.pop(0) for _ in range(k)]
    (x_ref,) = take(1)
    extra_ref = take(1)[0] if has_extra else None
    mix_in = take(len(mixer.ins)) if mixer else []
    g_ref, win_ref, wout_ref = take(3)
    fg_ref = take(1)[0] if final else None
    cast_src = take(n_cast)
    (o_ref,) = take(1)
    extra_o_ref = take(1)[0] if has_extra else None
    state = take(len(mixer.state_shapes)) if mixer else []
    cast_dst = take(n_cast)
    i = pl.program_id(0)
    step = i - n_load
    if n_load:
        win_s, wout_s = take(2)

        @pl.when(i < n_load)
        def _():
            for src, dst in ((win_ref, win_s), (wout_ref, wout_s)):
                rows = src.shape[0]
                dst[pl.ds(pl.multiple_of(i * rows, rows), rows), :] = src[...].astype(BF16)

        win_ref, wout_ref = win_s, wout_s
    (act_ref,) = take(1)
    mix_scratch = take(len(mixer.scratch)) if mixer else []
    ffn = functools.partial(_ffn_tile, g_ref=g_ref, win_ref=win_ref, wout_ref=wout_ref, fg_ref=fg_ref,
                            act_ref=act_ref, d_ff=d_ff, chunk=chunk)
    _cast_side(cast_src, cast_dst)

    def main_step():
        if mixer:
            (xm,) = refs
            xm[...] = mixer.tile_fn(x_ref, jax.lax.rem(step, tiles_per_seq), step == 0, mix_in, state,
                                    mix_scratch)
            o_ref[...] = ffn(xm)
        else:
            o_ref[...] = ffn(x_ref)

    if not (has_extra or n_load):
        main_step()
        return

    pl.when((step >= 0) & (step < n_tiles))(main_step)
    if has_extra:
        @pl.when(step == n_tiles)
        def _():
            extra_o_ref[...] = ffn(extra_ref)


def _cast_specs(w, sel, n_steps, first=0):
    r, c = w.shape[len(sel):]
    n_blocks = n_steps
    while r % n_blocks or (r // n_blocks) % BF16_SUBLANES:
        assert n_blocks % 2 == 0, (r, n_steps)
        n_blocks //= 2
    rep = n_steps // n_blocks
    rows = r // n_blocks
    blk = lambda i: jnp.clip(i - first, 0, n_steps - 1) // rep
    in_spec = pl.BlockSpec((None,) * len(sel) + (rows, c), lambda i: tuple(sel) + (blk(i), 0))
    out_spec = pl.BlockSpec((rows, c), lambda i: (blk(i), 0))
    return in_spec, out_spec, jax.ShapeDtypeStruct((r, c), BF16)


def _ffn(x, g, w_in, w_out, final_g=None, cast=(), mixer=None, extra=None):
    n, d = x.shape
    own_f32 = isinstance(w_in, tuple)
    n_load = WEIGHT_LOAD_STEPS if own_f32 else 0
    w_shapes = [w[0].shape[len(w[1]):] if own_f32 else w.shape for w in (w_in, w_out)]
    d_ff = w_shapes[1][0]
    tile = min(FFN_TILE, n)
    assert n % tile == 0 and d_ff % FFN_CHUNK == 0
    n_tiles = n // tile
    final = final_g is not None
    has_extra = extra is not None
    tiles_per_seq = 1
    if mixer:
        assert mixer.seq_len % tile == 0
        tiles_per_seq = mixer.seq_len // tile
    main_tile = lambda i: jnp.clip(i - n_load, 0, n_tiles - 1)
    x_spec = o_spec = pl.BlockSpec((tile, d), lambda i: (main_tile(i), 0))
    ins, specs = [x], [x_spec]
    out_shapes, out_specs = [jax.ShapeDtypeStruct((n, d), F32)], [o_spec]
    if has_extra:
        ins.append(extra)
        specs.append(pl.BlockSpec(extra.shape, lambda i: (0, 0)))
        out_shapes.append(jax.ShapeDtypeStruct(extra.shape, F32))
        out_specs.append(pl.BlockSpec(extra.shape, lambda i: (0, 0)))
    if mixer:
        ins += list(mixer.ins)
        specs += [_const_spec(a.shape) for a in mixer.ins]
    ins.append(g.reshape(1, d))
    specs.append(_const_spec((1, d)))
    scratch = []
    for w, (r, c) in zip((w_in, w_out), w_shapes):
        if own_f32:
            arr, sel = w
            assert r % n_load == 0 and (r // n_load) % BF16_SUBLANES == 0
            ins.append(arr)
            specs.append(pl.BlockSpec((None,) * len(sel) + (r // n_load, c),
                                      lambda i, sel=sel: tuple(sel) + (jnp.minimum(i, n_load - 1), 0)))
            scratch.append(pltpu.VMEM((r, c), BF16))
        else:
            ins.append(w)
            specs.append(_const_spec((r, c)))
    if final:
        ins.append(final_g.reshape(1, d))
        specs.append(_const_spec((1, d)))
    assert not has_extra or extra.shape[0] == tile
    scratch.append(pltpu.VMEM((tile, d_ff), BF16))
    if mixer:
        for shp in mixer.state_shapes:
            out_shapes.append(jax.ShapeDtypeStruct(shp, F32))
            out_specs.append(pl.BlockSpec((None,) + tuple(shp[1:]),
                                          lambda i: (main_tile(i) // tiles_per_seq, 0, 0)))
        scratch += list(mixer.scratch) + [pltpu.VMEM((tile, d), F32)]
    cast_ins, cast_specs_, cast_out_specs, cast_out_shapes = [], [], [], []
    for w, sel in cast:
        in_spec, out_spec, out_shape = _cast_specs(w, sel, n_tiles, first=n_load)
        cast_ins.append(w)
        cast_specs_.append(in_spec)
        cast_out_specs.append(out_spec)
        cast_out_shapes.append(out_shape)
    n_state = len(mixer.state_shapes) if mixer else 0
    outs = list(pl.pallas_call(
        functools.partial(_ffn_kernel, d_ff=d_ff, chunk=FFN_CHUNK, final=final, n_cast=len(cast),
                          mixer=mixer, tiles_per_seq=tiles_per_seq, n_tiles=n_tiles, has_extra=has_extra,
                          n_load=n_load),
        out_shape=tuple(out_shapes + cast_out_shapes),
        grid=(n_load + n_tiles + (1 if has_extra else 0),),
        in_specs=specs + cast_specs_,
        out_specs=tuple(out_specs + cast_out_specs),
        scratch_shapes=scratch,
        compiler_params=_params(("arbitrary",)),
        name=(mixer.name + "_" if mixer else "") + ("ffn_final" if final else "ffn"),
    )(*ins, *cast_ins))
    y = outs.pop(0)
    y_extra = outs.pop(0) if has_extra else None
    return y, y_extra, tuple(outs[:n_state]), tuple(outs[n_state:])


def _layernorm_silu(y, g, b):
    mu = jnp.mean(y, axis=-1, keepdims=True)
    yc = y - mu
    yn = yc * jax.lax.rsqrt(jnp.mean(yc * yc, axis=-1, keepdims=True) + EPS) * g + b
    return yn * jax.nn.sigmoid(yn)


def _split_z(z, d_a, d_b):
    a_val = z[:, :d_a]
    a_gate = z[:, d_a:2 * d_a]
    b_gate = z[:, 2 * d_a:2 * d_a + d_b]
    c_gate = z[:, 2 * d_a + d_b:2 * d_a + 2 * d_b]
    v = z[:, 2 * d_a + 2 * d_b:]
    return a_val, a_gate, b_gate, c_gate, v


A_PAD = 32
B_PAD = 8
SKEW_TAPS = CONV_A_WIDTH + SUBLANES - 1


def _mix_ab_tile(x_ref, t, first_step, in_refs, state_refs, scratch_refs, *, tile, tiles_per_seq, d_a, d_b):
    g_ref, win_ref, caw_ref, cab_ref, lng_ref, lnb_ref, cbw_ref, wout_ref = in_refs
    na_ref, nb_ref = state_refs
    exta, extb, ycat, wskew, bgate = scratch_refs

    @pl.when(first_step)
    def _():
        wskew[...] = jnp.zeros(wskew.shape, F32)
        for dd in range(SKEW_TAPS):
            for p in range(SUBLANES):
                if 0 <= dd - p < CONV_A_WIDTH:
                    wskew[dd, p:p + 1, :] = caw_ref[dd - p:dd - p + 1, :]

    @pl.when(t == 0)
    def _():
        exta[0:A_PAD, :] = jnp.zeros((A_PAD, d_a), F32)
        extb[0:B_PAD, :] = jnp.zeros((B_PAD, d_b), F32)

    h = _rms(x_ref[...], g_ref[...]).astype(BF16)
    za = _dot(h, win_ref[:, :2 * d_a])
    exta[A_PAD:A_PAD + tile, :] = za[:, :d_a] * jax.nn.sigmoid(za[:, d_a:])
    bgate[...] = _dot(h, win_ref[:, 2 * d_a:2 * d_a + d_b])
    zcv = _dot(h, win_ref[:, 2 * d_a + d_b:])
    extb[B_PAD:B_PAD + tile, :] = zcv[:, :d_b] * zcv[:, d_b:]

    off_a = A_PAD - (CONV_A_WIDTH - 1)
    off_b = B_PAD - (CONV_B_WIDTH - 1)
    for r0 in range(0, tile, CONV_ROWS):
        rs = slice(r0, r0 + CONV_ROWS)
        for c0 in range(0, d_a, LANES):
            cs = slice(c0, c0 + LANES)
            n_acc = CONV_ROWS // SUBLANES
            accs = [jnp.broadcast_to(cab_ref[:, cs], (SUBLANES, LANES))] * n_acc
            for dd in range(SKEW_TAPS):
                wv = wskew[dd, :, cs]
                for m in range(n_acc):
                    src = r0 + SUBLANES * m + off_a + dd
                    accs[m] = accs[m] + wv * jnp.broadcast_to(exta[src:src + 1, cs], (SUBLANES, LANES))
            for m in range(n_acc):
                ycat[r0 + SUBLANES * m:r0 + SUBLANES * (m + 1), cs] = accs[m]
        ycat[rs, :d_a] = _layernorm_silu(ycat[rs, :d_a], lng_ref[...], lnb_ref[...])
        for c0 in range(0, d_b, LANES):
            cs = slice(c0, c0 + LANES)
            acc = None
            for k in range(CONV_B_WIDTH):
                term = cbw_ref[k:k + 1, cs] * extb[r0 + off_b + k:r0 + off_b + k + CONV_ROWS, cs]
                acc = term if acc is None else acc + term
            ycat[rs, d_a + c0:d_a + c0 + LANES] = bgate[rs, cs] * acc

    out = _dot(ycat[...].astype(BF16), wout_ref[...])

    exta[0:A_PAD, :] = exta[tile:tile + A_PAD, :]
    extb[0:B_PAD, :] = extb[tile:tile + B_PAD, :]

    @pl.when(t == tiles_per_seq - 1)
    def _():
        na_ref[...] = exta[off_a:A_PAD, :]
        nb_ref[...] = extb[off_b:B_PAD, :]

    return x_ref[...] + out


def _mix_ab_prompt(n_seq, seq_len, g, w_in, caw, cab, lng, lnb, cbw, w_out):
    d_a = caw.shape[1]
    d_b = cbw.shape[1]
    tile = min(FFN_TILE, seq_len)
    assert tile >= A_PAD and tile % CONV_ROWS == 0
    row = lambda a: a.reshape(1, -1)
    return _Mixer(
        name="mix_ab",
        tile_fn=functools.partial(_mix_ab_tile, tile=tile, tiles_per_seq=seq_len // tile, d_a=d_a, d_b=d_b),
        ins=(row(g), w_in, caw, row(cab), row(lng), row(lnb), cbw, w_out),
        state_shapes=((n_seq, CONV_A_WIDTH - 1, d_a), (n_seq, CONV_B_WIDTH - 1, d_b)),
        scratch=(pltpu.VMEM((A_PAD + tile, d_a), F32), pltpu.VMEM((B_PAD + tile, d_b), F32),
                 pltpu.VMEM((tile, d_a + d_b), F32), pltpu.VMEM((SKEW_TAPS, SUBLANES, d_a), F32),
                 pltpu.VMEM((tile, d_b), F32)),
        seq_len=seq_len)


def _mix_ab_sample_kernel(x_ref, ha_ref, hb_ref, g_ref, win_ref, caw_ref, cab_ref, lng_ref, lnb_ref,
                          cbw_ref, wout_ref, o_ref, na_ref, nb_ref, ua_s, ub_s, ycat,
                          *, n_seq, t_new, d_a, d_b):
    hist_a = CONV_A_WIDTH - 1
    hist_b = CONV_B_WIDTH - 1

    def rows(t):
        return slice(t * n_seq, (t + 1) * n_seq)

    def ext_a(j):
        return ha_ref[j] if j < hist_a else ua_s[rows(j - hist_a), :]

    def ext_b(j):
        return hb_ref[j] if j < hist_b else ub_s[rows(j - hist_b), :]

    x = x_ref[...]
    h = _rms(x, g_ref[...]).astype(BF16)
    z = _dot(h, win_ref[...])
    a_val, a_gate, b_gate, c_gate, v = _split_z(z, d_a, d_b)
    ua_s[...] = a_val * jax.nn.sigmoid(a_gate)
    ub_s[...] = c_gate * v

    for t in range(t_new):
        acc = jnp.broadcast_to(cab_ref[...], (n_seq, d_a))
        for k in range(CONV_A_WIDTH):
            acc = acc + caw_ref[k:k + 1, :] * ext_a(t + k)
        ycat[rows(t), 0:d_a] = _layernorm_silu(acc, lng_ref[...], lnb_ref[...])
        acc = None
        for k in range(CONV_B_WIDTH):
            term = cbw_ref[k:k + 1, :] * ext_b(t + k)
            acc = term if acc is None else acc + term
        ycat[rows(t), d_a:d_a + d_b] = acc

    ya = ycat[:, :d_a]
    yb = b_gate * ycat[:, d_a:]
    out = _dot(ya.astype(BF16), wout_ref[:d_a, :]) + _dot(yb.astype(BF16), wout_ref[d_a:, :])
    o_ref[...] = x + out

    for j in range(hist_a):
        na_ref[j] = ext_a(j + t_new)
    for j in range(hist_b):
        nb_ref[j] = ext_b(j + t_new)


def _mix_ab_sample(x, hist_a, hist_b, g, w_in, caw, cab, lng, lnb, cbw, w_out):
    n, d = x.shape
    b = hist_a.shape[1]
    d_a = caw.shape[1]
    d_b = cbw.shape[1]
    row = lambda a: a.reshape(1, -1)
    ins = [x, hist_a, hist_b, row(g), w_in, caw, row(cab), row(lng), row(lnb), cbw, w_out]
    return pl.pallas_call(
        functools.partial(_mix_ab_sample_kernel, n_seq=b, t_new=n // b, d_a=d_a, d_b=d_b),
        out_shape=(jax.ShapeDtypeStruct((n, d), F32),
                   jax.ShapeDtypeStruct(hist_a.shape, F32),
                   jax.ShapeDtypeStruct(hist_b.shape, F32)),
        scratch_shapes=[pltpu.VMEM((n, d_a), F32), pltpu.VMEM((n, d_b), F32),
                        pltpu.VMEM((n, d_a + d_b), F32)],
        compiler_params=pltpu.CompilerParams(vmem_limit_bytes=VMEM_LIMIT_BYTES),
        name="mix_ab_sample",
    )(*ins)


P_PAD = 16


def _pool_mix(pooled, pw_ref, pb_ref, ps_ref, group):
    outs = []
    for gi in range(len(POOL_WINDOWS)):
        cs = slice(gi * group, (gi + 1) * group)
        outs.append(_dot(pooled[:, cs].astype(BF16), pw_ref[gi]))
    mixed = jnp.concatenate(outs, axis=-1)
    return (mixed + pb_ref[...]) * ps_ref[...]


def _pool_tile(x_ref, t, first_step, in_refs, state_refs, scratch_refs, *, tile, tiles_per_seq, d, group):
    del first_step
    g_ref, pw_ref, pb_ref, ps_ref = in_refs
    (np_ref,) = state_refs
    exth, pooled_s = scratch_refs

    @pl.when(t == 0)
    def _():
        exth[0:P_PAD, :] = jnp.zeros((P_PAD, d), F32)

    exth[P_PAD:P_PAD + tile, :] = _rms(x_ref[...], g_ref[...])

    pos = (t * tile + jax.lax.broadcasted_iota(jnp.int32, (tile, group), 0)).astype(F32)
    for gi, w in enumerate(POOL_WINDOWS):
        assert w & (w - 1) == 0 and w <= P_PAD
        cs = slice(gi * group, (gi + 1) * group)
        s = exth[:, cs]
        shift = 1
        while shift < w:
            s = s + pltpu.roll(s, shift, 0)
            shift *= 2
        count = jnp.minimum(pos + 1.0, float(w))
        pooled_s[:, cs] = s[P_PAD:, :] / count - exth[P_PAD:P_PAD + tile, cs]

    y = x_ref[...] + _pool_mix(pooled_s[...], pw_ref, pb_ref, ps_ref, group)

    exth[0:P_PAD, :] = exth[tile:tile + P_PAD, :]

    @pl.when(t == tiles_per_seq - 1)
    def _():
        np_ref[...] = exth[P_PAD - POOL_HIST:P_PAD, :]

    return y


def _pool_prompt(n_seq, seq_len, d, g, pw, pb, ps):
    group = pw.shape[-1]
    tile = min(FFN_TILE, seq_len)
    assert tile >= P_PAD
    row = lambda a: a.reshape(1, -1)
    return _Mixer(
        name="pool",
        tile_fn=functools.partial(_pool_tile, tile=tile, tiles_per_seq=seq_len // tile, d=d, group=group),
        ins=(row(g), pw, row(pb), row(ps)),
        state_shapes=((n_seq, POOL_HIST, d),),
        scratch=(pltpu.VMEM((P_PAD + tile, d), F32), pltpu.VMEM((tile, d), F32)),
        seq_len=seq_len)


def _pool_sample_kernel(x_ref, hp_ref, g_ref, pw_ref, pb_ref, ps_ref, o_ref, np_ref, h_s, pooled_s,
                        *, n_seq, t_new, pos0, group):
    def rows(t):
        return slice(t * n_seq, (t + 1) * n_seq)

    def ext(j, cs):
        return hp_ref[j, :, cs] if j < POOL_HIST else h_s[rows(j - POOL_HIST), cs]

    x = x_ref[...]
    h_s[...] = _rms(x, g_ref[...])

    for t in range(t_new):
        for gi, w in enumerate(POOL_WINDOWS):
            cs = slice(gi * group, (gi + 1) * group)
            s = ext(POOL_HIST + t, cs)
            for j in range(1, w):
                s = s + ext(POOL_HIST + t - j, cs)
            count = float(min(pos0 + t + 1, w))
            pooled_s[rows(t), cs] = s / count - ext(POOL_HIST + t, cs)

    o_ref[...] = x + _pool_mix(pooled_s[...], pw_ref, pb_ref, ps_ref, group)

    full = slice(None)
    for j in range(POOL_HIST):
        np_ref[j] = ext(j + t_new, full)


def _pool_sample(x, hist, pos0, g, pw, pb, ps):
    n, d = x.shape
    b = hist.shape[1]
    group = pw.shape[-1]
    row = lambda a: a.reshape(1, -1)
    return pl.pallas_call(
        functools.partial(_pool_sample_kernel, n_seq=b, t_new=n // b, pos0=pos0, group=group),
        out_shape=(jax.ShapeDtypeStruct((n, d), F32), jax.ShapeDtypeStruct(hist.shape, F32)),
        scratch_shapes=[pltpu.VMEM((n, d), F32), pltpu.VMEM((n, d), F32)],
        compiler_params=pltpu.CompilerParams(vmem_limit_bytes=VMEM_LIMIT_BYTES),
        name="pool_sample",
    )(x, hist, row(g), pw, row(pb), row(ps))


def kernel(x_prompt, x_sample, state_conv_a, state_conv_b, state_pool, norm_g, ffn_w_in, ffn_w_out,
           mix_w_in, conv_a_w, conv_a_b, ln_a_g, ln_a_b, conv_b_w, mix_w_out, pool_w, pool_b,
           pool_scale, final_norm_g):
    depth = norm_g.shape[0]
    d = x_prompt.shape[-1]
    bs, ts, _ = x_sample.shape
    step_major = lambda a: jnp.swapaxes(a, 0, 1)

    n_even, n_odd = mix_w_in.shape[0], pool_w.shape[0]
    n_groups, group = pool_w.shape[1], pool_w.shape[2]
    pool_w2 = pool_w.reshape(n_odd, n_groups * group, group)
    mixer_casts = ([(mix_w_in, (e,)) for e in range(n_even)] + [(mix_w_out, (e,)) for e in range(n_even)]
                   + [(pool_w2, (o,)) for o in range(n_odd)])
    ffn_w = {(0, 0): ((ffn_w_in, (0, 0)), (ffn_w_out, (0, 0)))}
    mix_w_in_h, mix_w_out_h, pool_w_h = [], [], []

    def ffn(xp, xs, l, which, final_g=None, mixer=None):
        nxt = (l, 1) if which == 0 else (l + 1, 0)
        cast = [(ffn_w_in, nxt), (ffn_w_out, nxt)] if nxt[0] < depth else []
        if (l, which) == (0, 0):
            cast = cast + mixer_casts
        yp, ys, states, casts = _ffn(xp, norm_g[l, 2 * which], *ffn_w[(l, which)], final_g, cast, mixer,
                                     extra=xs)
        if nxt[0] < depth:
            ffn_w[nxt] = casts[:2]
        if (l, which) == (0, 0):
            rest = casts[len(casts) - len(mixer_casts):]
            mix_w_in_h.extend(rest[:n_even])
            mix_w_out_h.extend(rest[n_even:2 * n_even])
            pool_w_h.extend(w.reshape(n_groups, group, group) for w in rest[2 * n_even:])
        return yp, ys, states

    bp, tp, _ = x_prompt.shape
    xp, xs = x_prompt.reshape(bp * tp, d), step_major(x_sample).reshape(ts * bs, d)
    pa, pb, pp, sa, sb, sp = [], [], [], [], [], []
    for l in range(depth):
        final_g = final_norm_g if l == depth - 1 else None
        xp, xs, _ = ffn(xp, xs, l, 0)
        if l % 2 == 0:
            e = l // 2
            wts = (norm_g[l, 1], mix_w_in_h[e], conv_a_w[e], conv_a_b[e], ln_a_g[e], ln_a_b[e],
                   conv_b_w[e], mix_w_out_h[e])
            xs, na, nb = _mix_ab_sample(xs, step_major(state_conv_a[e]), step_major(state_conv_b[e]),
                                        *wts)
            sa.append(step_major(na))
            sb.append(step_major(nb))
            xp, xs, (na, nb) = ffn(xp, xs, l, 1, final_g, _mix_ab_prompt(bp, tp, *wts))
            pa.append(na)
            pb.append(nb)
        else:
            o = l // 2
            wts = (norm_g[l, 1], pool_w_h[o], pool_b[o], pool_scale[o])
            xs, npool = _pool_sample(xs, step_major(state_pool[o]), PAST_LEN, *wts)
            sp.append(step_major(npool))
            xp, xs, (npool,) = ffn(xp, xs, l, 1, final_g, _pool_prompt(bp, tp, d, *wts))
            pp.append(npool)

    xp = xp.reshape(bp, tp, d)
    xs = step_major(xs.reshape(ts, bs, d))
    st = lambda parts: jnp.stack(parts, axis=0)
    return (xp, xs, st(pa), st(pb), st(pp), st(sa), st(sb), st(sp))
```
